```python
import jax
import jax.numpy as jnp
from jax import lax
import numpy as np

D_MODEL = 1024
BATCH = 8
SEQ = 2048
DEPTH = 2
DEC_BATCH = 32
DEC_SEQ = 8
PAST_LEN = 8192
PAGE_SIZE = 128

ATTN_WIDTH = D_MODEL // 2
N_HEADS = 8
HEAD_DIM = ATTN_WIDTH // N_HEADS
CONV_WIDTH = D_MODEL - ATTN_WIDTH
CONV_K = 31
MIX_WIDTH = ATTN_WIDTH + CONV_WIDTH
IN_COLS = 3 * ATTN_WIDTH + N_HEADS + 2 * CONV_WIDTH
Q_BLOCK = 128
FORGET_BIAS_LO = 1.0
FORGET_BIAS_HI = 9.0
N_MEM = 256
X_HEADS = 4
X_HEAD_DIM = D_MODEL // 8
X_WIDTH = X_HEADS * X_HEAD_DIM
D_FF = ((8 * D_MODEL // 3 + 127) // 128) * 128
N_EXPERTS = 8
TOP_K = 2
E_FF = 7 * D_MODEL // 2
N_DENSE = (DEPTH + 1) // 2
N_MOE = DEPTH // 2
EPS = 1e-6

kernel_name = 'fox_conformer_parallel_heads_decode_step'


def rmsnorm(x, g):
    xf = x.astype(jnp.float32)
    y = xf * lax.rsqrt(jnp.mean(xf * xf, axis=-1, keepdims=True) + EPS)
    return (y * g.astype(jnp.float32)).astype(x.dtype)


def layernorm(x, g, b):
    xf = x.astype(jnp.float32)
    xc = xf - jnp.mean(xf, axis=-1, keepdims=True)
    y = xc * lax.rsqrt(jnp.mean(xc * xc, axis=-1, keepdims=True) + EPS)
    return (y * g.astype(jnp.float32) + b.astype(jnp.float32)).astype(x.dtype)


def causal_dwconv(u_hist, w, b):
    out = lax.conv_general_dilated(u_hist, w[:, None, :].astype(u_hist.dtype), window_strides=(1,),
                                   padding='VALID', dimension_numbers=('NWC', 'WIO', 'NWC'),
                                   feature_group_count=u_hist.shape[-1])
    return out + b.astype(out.dtype)


def fox_prompt(q, k, v, c):
    n, s, h, dh = q.shape
    nb = s // Q_BLOCK
    scale = HEAD_DIM ** -0.5
    qb = q.reshape(n, nb, Q_BLOCK, h, dh).swapaxes(0, 1)
    cb = c.reshape(n, nb, Q_BLOCK, h).swapaxes(0, 1)
    ck = c.transpose(0, 2, 1)[:, :, None, :]
    kpos = jnp.arange(s)

    def block(args):
        qi, ci, i = args
        sc = jnp.einsum('bqhd,bkhd->bhqk', qi, k).astype(jnp.float32) * scale
        sc = sc + ci.transpose(0, 2, 1)[..., None] - ck
        qpos = i * Q_BLOCK + jnp.arange(Q_BLOCK)
        sc = jnp.where(kpos[None, :] <= qpos[:, None], sc, -jnp.inf)
        p = jax.nn.softmax(sc, axis=-1).astype(v.dtype)
        return jnp.einsum('bhqk,bkhd->bqhd', p, v)

    o = lax.map(block, (qb, cb, jnp.arange(nb)))
    return o.swapaxes(0, 1).reshape(n, s, h * dh)


def fox_sample(q, k, v, logf, k_past, v_past, lf_past):
    n, t = q.shape[0], q.shape[1]
    p_len = k_past.shape[1]
    scale = HEAD_DIM ** -0.5
    c_new = jnp.cumsum(logf, axis=1).transpose(0, 2, 1)
    d_past = (lax.cumsum(lf_past, axis=1, reverse=True) - lf_past).transpose(0, 2, 1)
    s_past = (jnp.einsum('bthd,bshd->bhts', q, k_past).astype(jnp.float32) * scale
              + c_new[..., None] + d_past[:, :, None, :])
    s_new = (jnp.einsum('bthd,bshd->bhts', q, k).astype(jnp.float32) * scale
             + c_new[..., None] - c_new[:, :, None, :])
    s_new = jnp.where(jnp.tril(jnp.ones((t, t), dtype=bool)), s_new, -jnp.inf)
    p = jax.nn.softmax(jnp.concatenate([s_past, s_new], axis=-1), axis=-1)
    o = (jnp.einsum('bhts,bshd->bthd', p[..., :p_len].astype(v_past.dtype), v_past)
         + jnp.einsum('bhts,bshd->bthd', p[..., p_len:].astype(v.dtype), v))
    return o.reshape(n, t, ATTN_WIDTH)


def cross_attend(h, mk, mv, w_cq, w_co):
    n, t, _ = h.shape
    q = (h @ w_cq).reshape(n, t, X_HEADS, X_HEAD_DIM)
    s = jnp.einsum('bthd,bmhd->bhtm', q, mk.astype(q.dtype)).astype(jnp.float32) * (X_HEAD_DIM ** -0.5)
    p = jax.nn.softmax(s, axis=-1).astype(mv.dtype)
    o = jnp.einsum('bhtm,bmhd->bthd', p, mv).reshape(n, t, X_WIDTH)
    return o @ w_co


def memory_kv(mem, g_mem, w_ckv):
    n, m, _ = mem.shape
    kv = rmsnorm(mem, g_mem) @ w_ckv
    mk = kv[..., :X_WIDTH].reshape(n, m, X_HEADS, X_HEAD_DIM)
    mv = kv[..., X_WIDTH:].reshape(n, m, X_HEADS, X_HEAD_DIM)
    return mk, mv


def swiglu(h, w_gu, w_down):
    d = w_down.shape[0]
    gu = h @ w_gu
    return (jax.nn.silu(gu[..., :d]) * gu[..., d:]) @ w_down


def moe_swiglu(h, w_router, b_router, w_e_gu, w_e_down):
    logits = (h @ w_router).astype(jnp.float32) + b_router.astype(jnp.float32)
    probs = jax.nn.softmax(logits, axis=-1)
    top_v, top_i = lax.top_k(probs, TOP_K)
    top_v = top_v / jnp.sum(top_v, axis=-1, keepdims=True)
    gates = jnp.sum(jax.nn.one_hot(top_i, N_EXPERTS, dtype=jnp.float32) * top_v[..., None], axis=-2)
    gates = gates.astype(h.dtype)
    y = jnp.zeros_like(h)
    for e in range(N_EXPERTS):
        y = y + gates[..., e:e + 1] * swiglu(h, w_e_gu[e], w_e_down[e])
    return y


def mixer_projections(x, P, l):
    n, t, _ = x.shape
    a = ATTN_WIDTH
    proj = rmsnorm(x, P['g_mix'][l]) @ P['w_in'][l]
    q = proj[..., :a].reshape(n, t, N_HEADS, HEAD_DIM)
    k = proj[..., a:2 * a].reshape(n, t, N_HEADS, HEAD_DIM)
    v = proj[..., 2 * a:3 * a].reshape(n, t, N_HEADS, HEAD_DIM)
    f_logit = proj[..., 3 * a:3 * a + N_HEADS]
    glu = proj[..., 3 * a + N_HEADS:]
    logf = jax.nn.log_sigmoid(f_logit.astype(jnp.float32) + P['b_fgate'][l].astype(jnp.float32))
    u = glu[..., :CONV_WIDTH] * jax.nn.sigmoid(glu[..., CONV_WIDTH:])
    return q, k, v, logf, u


def layer_rest(x, o_attn, u_hist, mk, mv, P, l):
    z = causal_dwconv(u_hist, P['w_dw'][l], P['b_dw'][l])
    o_conv = jax.nn.silu(layernorm(z, P['g_cln'][l], P['b_cln'][l]))
    merged = jnp.concatenate([rmsnorm(o_attn, P['g_attn_out'][l]), rmsnorm(o_conv, P['g_conv_out'][l])], axis=-1)
    x = x + merged @ P['w_out'][l]
    x = x + cross_attend(rmsnorm(x, P['g_cross'][l]), mk, mv, P['w_cq'][l], P['w_co'][l])
    h = rmsnorm(x, P['g_ffn'][l])
    if l % 2 == 0:
        x = x + swiglu(h, P['w_ff_gu'][l // 2], P['w_ff_down'][l // 2])
    else:
        x = x + moe_swiglu(h, P['w_router'][l // 2], P['b_router'][l // 2], P['w_e_gu'][l // 2], P['w_e_down'][l // 2])
    return x


def setup_inputs(seed: int = 0) -> dict:
    key = jax.random.key(seed)
    keys = list(jax.random.split(key, 40))

    def nrm(shape, scale):
        return jax.random.normal(keys.pop(), shape, jnp.float32) * scale

    def gain(shape):
        return 1.0 + nrm(shape, 0.02)

    n_pages = PAST_LEN // PAGE_SIZE
    n_used = DEC_BATCH * n_pages
    n_phys = n_used + max(1, n_used // 4)
    perm = jax.random.permutation(keys.pop(), n_phys)
    page_table = perm[:n_used].reshape(DEC_BATCH, n_pages).astype(jnp.int32)
    head_bias = jnp.linspace(FORGET_BIAS_LO, FORGET_BIAS_HI, N_HEADS, dtype=jnp.float32)
    D = D_MODEL
    return {
        'x_prompt': nrm((BATCH, SEQ, D), 1.0),
        'x_sample': nrm((DEC_BATCH, DEC_SEQ, D), 1.0),
        'cache_k': nrm((DEPTH, n_phys, PAGE_SIZE, N_HEADS, HEAD_DIM), 1.0),
        'cache_v': nrm((DEPTH, n_phys, PAGE_SIZE, N_HEADS, HEAD_DIM), 1.0),
        'cache_logf': jax.nn.log_sigmoid(head_bias + nrm((DEPTH, n_phys, PAGE_SIZE, N_HEADS), 1.0)),
        'cache_conv': nrm((DEPTH, DEC_BATCH, CONV_K - 1, CONV_WIDTH), 0.5),
        'cache_mem_k': nrm((DEPTH, DEC_BATCH, N_MEM, X_HEADS, X_HEAD_DIM), 1.0),
        'cache_mem_v': nrm((DEPTH, DEC_BATCH, N_MEM, X_HEADS, X_HEAD_DIM), 1.0),
        'page_table': page_table,
        'mem_prompt': nrm((BATCH, N_MEM, D), 1.0),
        'g_mix': gain((DEPTH, D)),
        'w_in': nrm((DEPTH, D, IN_COLS), D ** -0.5),
        'b_fgate': head_bias + nrm((DEPTH, N_HEADS), 0.1),
        'w_dw': nrm((DEPTH, CONV_K, CONV_WIDTH), CONV_K ** -0.5),
        'b_dw': nrm((DEPTH, CONV_WIDTH), 0.02),
        'g_cln': gain((DEPTH, CONV_WIDTH)),
        'b_cln': nrm((DEPTH, CONV_WIDTH), 0.02),
        'g_attn_out': gain((DEPTH, ATTN_WIDTH)),
        'g_conv_out': gain((DEPTH, CONV_WIDTH)),
        'w_out': nrm((DEPTH, MIX_WIDTH, D), MIX_WIDTH ** -0.5),
        'g_cross': gain((DEPTH, D)),
        'g_mem': gain((DEPTH, D)),
        'w_cq': nrm((DEPTH, D, X_WIDTH), D ** -0.5),
        'w_ckv': nrm((DEPTH, D, 2 * X_WIDTH), D ** -0.5),
        'w_co': nrm((DEPTH, X_WIDTH, D), X_WIDTH ** -0.5),
        'g_ffn': gain((DEPTH, D)),
        'w_ff_gu': nrm((N_DENSE, D, 2 * D_FF), D ** -0.5),
        'w_ff_down': nrm((N_DENSE, D_FF, D), D_FF ** -0.5),
        'w_router': nrm((N_MOE, D, N_EXPERTS), D ** -0.5),
        'b_router': nrm((N_MOE, N_EXPERTS), 0.01),
        'w_e_gu': nrm((N_MOE, N_EXPERTS, D, 2 * E_FF), D ** -0.5),
        'w_e_down': nrm((N_MOE, N_EXPERTS, E_FF, D), E_FF ** -0.5),
        'g_final': gain((D,)),
    }


def reference(x_prompt, x_sample, cache_k, cache_v, cache_logf, cache_conv, cache_mem_k, cache_mem_v,
              page_table, mem_prompt, g_mix, w_in, b_fgate, w_dw, b_dw, g_cln, b_cln, g_attn_out,
              g_conv_out, w_out, g_cross, g_mem, w_cq, w_ckv, w_co, g_ffn, w_ff_gu, w_ff_down,
              w_router, b_router, w_e_gu, w_e_down, g_final):
    P = dict(g_mix=g_mix, w_in=w_in, b_fgate=b_fgate, w_dw=w_dw, b_dw=b_dw, g_cln=g_cln, b_cln=b_cln,
             g_attn_out=g_attn_out, g_conv_out=g_conv_out, w_out=w_out, g_cross=g_cross, w_cq=w_cq,
             w_co=w_co, g_ffn=g_ffn, w_ff_gu=w_ff_gu, w_ff_down=w_ff_down, w_router=w_router,
             b_router=b_router, w_e_gu=w_e_gu, w_e_down=w_e_down)

    x = x_prompt
    kp, vp, lfp, cvp, mkp, mvp = [], [], [], [], [], []
    for l in range(DEPTH):
        q, k, v, logf, u = mixer_projections(x, P, l)
        o_attn = fox_prompt(q, k, v, jnp.cumsum(logf, axis=1))
        u_hist = jnp.pad(u, ((0, 0), (CONV_K - 1, 0), (0, 0)))
        mk, mv = memory_kv(mem_prompt, g_mem[l], w_ckv[l])
        x = layer_rest(x, o_attn, u_hist, mk, mv, P, l)
        kp.append(k)
        vp.append(v)
        lfp.append(logf)
        cvp.append(u_hist[:, -(CONV_K - 1):])
        mkp.append(mk)
        mvp.append(mv)
    y_prompt = rmsnorm(x, g_final)

    x = x_sample
    n_dec = x_sample.shape[0]
    ks, vs, lfs, cvs = [], [], [], []
    for l in range(DEPTH):
        q, k, v, logf, u = mixer_projections(x, P, l)
        k_past = cache_k[l][page_table].reshape(n_dec, -1, N_HEADS, HEAD_DIM)
        v_past = cache_v[l][page_table].reshape(n_dec, -1, N_HEADS, HEAD_DIM)
        lf_past = cache_logf[l][page_table].reshape(n_dec, -1, N_HEADS).astype(jnp.float32)
        o_attn = fox_sample(q, k, v, logf, k_past, v_past, lf_past)
        u_hist = jnp.concatenate([cache_conv[l].astype(u.dtype), u], axis=1)
        x = layer_rest(x, o_attn, u_hist, cache_mem_k[l], cache_mem_v[l], P, l)
        ks.append(k)
        vs.append(v)
        lfs.append(logf)
        cvs.append(u_hist[:, -(CONV_K - 1):])
    y_sample = rmsnorm(x, g_final)

    return (y_prompt, y_sample,
            jnp.stack(kp), jnp.stack(vp), jnp.stack(lfp), jnp.stack(cvp), jnp.stack(mkp), jnp.stack(mvp),
            jnp.stack(ks), jnp.stack(vs), jnp.stack(lfs), jnp.stack(cvs))
```

```python
import functools

import jax
import jax.numpy as jnp
from jax import lax
from jax.experimental import pallas as pl
from jax.experimental.pallas import tpu as pltpu

EPS = 1e-6
N_HEADS = 8
HEAD_DIM = 64
ATTN_WIDTH = N_HEADS * HEAD_DIM
CONV_K = 31
CONV_HALO = 32
X_HEADS = 4
X_HEAD_DIM = 128
X_WIDTH = X_HEADS * X_HEAD_DIM
N_EXPERTS = 8
LANES = 128
NEG = -1e30
VMEM_LIMIT_BYTES = 56 * 1024 * 1024
PAGES_PER_STEP = 8

F32 = jnp.float32
BF16 = jnp.bfloat16
HIGHEST = lax.Precision.HIGHEST


def _params(*sem):
    return pltpu.CompilerParams(dimension_semantics=sem, vmem_limit_bytes=VMEM_LIMIT_BYTES)


def _rms(x, g):
    return x * lax.rsqrt(jnp.mean(x * x, axis=-1, keepdims=True) + EPS) * g


def _sigmoid(x):
    return 1.0 / (1.0 + jnp.exp(-x))


def _dot(a, b):
    return jnp.dot(a, b, preferred_element_type=F32)


def _dot_nt(a, b):
    return lax.dot_general(a, b, (((1,), (1,)), ((), ())), preferred_element_type=F32)


def _full(shape):
    nd = len(shape)
    return pl.BlockSpec(shape, lambda *_: (0,) * nd)


def _proj_in_kernel(x_ref, g_ref, wqkv_ref, wf_ref, wglu_ref, bf_ref,
                    q_ref, k_ref, v_ref, kb_ref, vb_ref, lf_ref, u_ref):
    a = ATTN_WIDTH
    h = _rms(x_ref[...], g_ref[...]).astype(BF16)
    qkv = _dot(h, wqkv_ref[...])
    q_ref[...] = (qkv[:, :a] * (HEAD_DIM ** -0.5)).astype(BF16)
    k = qkv[:, a:2 * a]
    v = qkv[:, 2 * a:]
    k_ref[...] = k
    v_ref[...] = v
    kb_ref[...] = k.astype(BF16)
    vb_ref[...] = v.astype(BF16)
    z = _dot(h, wf_ref[...]) + bf_ref[...]
    lf_ref[...] = jnp.minimum(z, 0.0) - jnp.log1p(jnp.exp(-jnp.abs(z)))
    glu = _dot(h, wglu_ref[...])
    c = glu.shape[1] // 2
    u_ref[...] = glu[:, :c] * _sigmoid(glu[:, c:])


def _proj_in(x, g, wqkv, wf, wglu, bf, tm):
    m, d = x.shape
    a = ATTN_WIDTH
    c = wglu.shape[1] // 2
    row = lambda w: pl.BlockSpec((tm, w), lambda i: (i, 0))
    return pl.pallas_call(
        _proj_in_kernel,
        grid=(m // tm,),
        in_specs=[row(d), _full((1, d)), _full(wqkv.shape), _full(wf.shape), _full(wglu.shape),
                  _full((1, LANES))],
        out_specs=[row(a), row(a), row(a), row(a), row(a), row(LANES), row(c)],
        out_shape=[jax.ShapeDtypeStruct((m, a), BF16), jax.ShapeDtypeStruct((m, a), F32),
                   jax.ShapeDtypeStruct((m, a), F32), jax.ShapeDtypeStruct((m, a), BF16),
                   jax.ShapeDtypeStruct((m, a), BF16), jax.ShapeDtypeStruct((m, LANES), F32),
                   jax.ShapeDtypeStruct((m, c), F32)],
        compiler_params=_params("parallel"),
        name="proj_in",
    )(x, g, wqkv, wf, wglu, bf)


def _cumsum_kernel(lf_ref, c_ref):
    s = lf_ref.shape[0]
    lft = lf_ref[...].T[:N_HEADS, :]
    r = lax.broadcasted_iota(jnp.int32, (LANES, LANES), 0)
    cidx = lax.broadcasted_iota(jnp.int32, (LANES, LANES), 1)
    tri = (r <= cidx).astype(F32)
    carry = jnp.zeros((N_HEADS, 1), F32)
    for b in range(s // LANES):
        cb = jnp.dot(lft[:, b * LANES:(b + 1) * LANES], tri, precision=HIGHEST,
                     preferred_element_type=F32) + carry
        c_ref[0, :, b * LANES:(b + 1) * LANES] = cb
        carry = cb[:, LANES - 1:LANES]


def _cumsum_logf(lf_pad, nb, s):
    return pl.pallas_call(
        _cumsum_kernel,
        grid=(nb,),
        in_specs=[pl.BlockSpec((s, LANES), lambda b: (b, 0))],
        out_specs=pl.BlockSpec((1, N_HEADS, s), lambda b: (b, 0, 0)),
        out_shape=jax.ShapeDtypeStruct((nb, N_HEADS, s), F32),
        compiler_params=_params("parallel"),
        name="cumsum_logf",
    )(lf_pad)


def _fox_prompt_kernel(q_ref, k_ref, v_ref, c_ref, o_ref, m_scr, l_scr, acc_scr, *, t):
    i = pl.program_id(2)
    lane = lax.broadcasted_iota(jnp.int32, (1, LANES), 1)
    q = q_ref[...]
    zero = jnp.zeros_like(q)
    qh = (jnp.where(lane < HEAD_DIM, q, zero), jnp.where(lane >= HEAD_DIM, q, zero))
    m_scr[...] = jnp.full(m_scr.shape, NEG, F32)
    l_scr[...] = jnp.zeros(l_scr.shape, F32)
    acc_scr[...] = jnp.zeros(acc_scr.shape, F32)
    row = lax.broadcasted_iota(jnp.int32, (t, t), 0)
    col = lax.broadcasted_iota(jnp.int32, (t, t), 1)

    def step(j, masked):
        start = pl.multiple_of(j * t, t)
        kb = k_ref[pl.ds(start, t), :]
        vb = v_ref[pl.ds(start, t), :]
        cs = c_ref[:, j]
        for hh in range(2):
            s = _dot_nt(qh[hh], kb) - cs[hh]
            if masked:
                s = jnp.where(col <= row, s, NEG)
            m_prev = m_scr[hh]
            m_new = jnp.maximum(m_prev, jnp.max(s, axis=-1, keepdims=True))
            alpha = jnp.exp(m_prev - m_new)
            p = jnp.exp(s - m_new)
            l_scr[hh] = alpha * l_scr[hh] + jnp.sum(p, axis=-1, keepdims=True)
            acc_scr[hh] = alpha * acc_scr[hh] + _dot(p.astype(BF16), vb)
            m_scr[hh] = m_new

    def body(j, carry):
        step(j, False)
        return carry

    lax.fori_loop(0, i, body, 0)
    step(i, True)
    o_ref[...] = jnp.where(lane < HEAD_DIM, acc_scr[0] / l_scr[0], acc_scr[1] / l_scr[1])


def _fox_prompt(q, kb, vb, c, nb, s, t):
    m = q.shape[0]
    nt = s // t
    c4 = c.reshape(nb * N_HEADS, nt, 1, t)
    pairs = N_HEADS // 2
    return pl.pallas_call(
        functools.partial(_fox_prompt_kernel, t=t),
        grid=(nb, pairs, nt),
        in_specs=[pl.BlockSpec((t, LANES), lambda b, p, i: (b * nt + i, p)),
                  pl.BlockSpec((s, LANES), lambda b, p, i: (b, p)),
                  pl.BlockSpec((s, LANES), lambda b, p, i: (b, p)),
                  pl.BlockSpec((2, nt, 1, t), lambda b, p, i: (b * pairs + p, 0, 0, 0))],
        out_specs=pl.BlockSpec((t, LANES), lambda b, p, i: (b * nt + i, p)),
        out_shape=jax.ShapeDtypeStruct((m, ATTN_WIDTH), F32),
        scratch_shapes=[pltpu.VMEM((2, t, 1), F32), pltpu.VMEM((2, t, 1), F32),
                        pltpu.VMEM((2, t, LANES), F32)],
        compiler_params=_params("parallel", "parallel", "arbitrary"),
        name="fox_prompt",
    )(q, kb, vb, c4)


def _fox_sample_kernel(pt_ref, q_ref, kn_ref, vn_ref, lfn_ref, *rest, t_new, n_chunks):
    pg = PAGES_PER_STEP
    k_refs, v_refs, lf_refs = rest[:pg], rest[pg:2 * pg], rest[2 * pg:3 * pg]
    o_ref, m_scr, l_scr, acc_scr, carry_scr = rest[3 * pg:]
    del pt_ref
    j = pl.program_id(1)
    rows = N_HEADS * t_new
    width = ATTN_WIDTH
    page = lf_refs[0].shape[2]

    @pl.when(j == 0)
    def _():
        m_scr[...] = jnp.full(m_scr.shape, NEG, F32)
        l_scr[...] = jnp.zeros(l_scr.shape, F32)
        acc_scr[...] = jnp.zeros(acc_scr.shape, F32)
        carry_scr[...] = jnp.zeros(carry_scr.shape, F32)

    q = q_ref[...]
    q_rows = jnp.concatenate([q] * N_HEADS, axis=0)
    rr = lax.broadcasted_iota(jnp.int32, (rows, width), 0)
    ll = lax.broadcasted_iota(jnp.int32, (rows, width), 1)
    head_mask = (rr // t_new) == (ll // HEAD_DIM)
    q_rows = jnp.where(head_mask, q_rows, jnp.zeros_like(q_rows))

    def per_query_rows(x):
        return jnp.concatenate([jnp.broadcast_to(x[h:h + 1], (t_new, x.shape[1])) for h in range(N_HEADS)],
                               axis=0)

    def attend(kt, vt, bias):
        s = _dot(q_rows, kt) + bias
        m_prev = m_scr[...]
        m_new = jnp.maximum(m_prev, jnp.max(s, axis=-1, keepdims=True))
        alpha = jnp.exp(m_prev - m_new)
        p = jnp.exp(s - m_new)
        l_scr[...] = alpha * l_scr[...] + jnp.sum(p, axis=-1, keepdims=True)
        acc_scr[...] = alpha * acc_scr[...] + _dot_nt(p.astype(BF16), vt)
        m_scr[...] = m_new

    lane = lax.broadcasted_iota(jnp.int32, (N_HEADS, page), 1)
    later = carry_scr[...]
    for g in reversed(range(pg)):
        lf = lf_refs[g][0]
        suffix = lf
        shift = 1
        while shift < page:
            nxt = pltpu.roll(suffix, page - shift, axis=1)
            suffix = suffix + jnp.where(lane + shift < page, nxt, 0.0)
            shift *= 2
        bias = per_query_rows(suffix - lf + later)
        attend(k_refs[g][0].astype(BF16), v_refs[g][0].astype(BF16), bias)
        later = later + jnp.sum(lf, axis=-1, keepdims=True)
    carry_scr[...] = later

    @pl.when(j == n_chunks - 1)
    def _():
        c_new = lfn_ref[0]
        shift = 1
        while shift < t_new:
            c_new = c_new + jnp.where(lane >= shift, pltpu.roll(c_new, shift, axis=1), 0.0)
            shift *= 2
        br = lax.broadcasted_iota(jnp.int32, (rows, page), 0)
        bc = lax.broadcasted_iota(jnp.int32, (rows, page), 1)
        bias = jnp.where(bc <= (br % t_new), -per_query_rows(c_new), NEG)
        attend(kn_ref[...].astype(BF16), vn_ref[...].astype(BF16), bias)
        o = jnp.where(head_mask, acc_scr[...] / l_scr[...], 0.0)
        out = o[0:t_new]
        for h in range(1, N_HEADS):
            out = out + o[h * t_new:(h + 1) * t_new]
        o_ref[...] = out


def _fox_sample(page_table, base, q, kt_new, vt_new, lft_new, cache_kt, cache_vt, cache_lft, t_new):
    n, n_pages = page_table.shape
    pg = PAGES_PER_STEP
    n_chunks = n_pages // pg
    assert n_pages % pg == 0
    width, page = cache_kt.shape[1], cache_kt.shape[2]
    rows = N_HEADS * t_new

    def page_spec(shape, g):
        return pl.BlockSpec(shape, lambda b, j, pt: (base + pt[b, (n_chunks - 1 - j) * pg + g], 0, 0))

    in_specs = ([pl.BlockSpec((t_new, width), lambda b, j, pt: (b, 0)),
                 pl.BlockSpec((width, page), lambda b, j, pt: (b, 0)),
                 pl.BlockSpec((width, page), lambda b, j, pt: (b, 0)),
                 pl.BlockSpec((1, N_HEADS, page), lambda b, j, pt: (b, 0, 0))]
                + [page_spec((1, width, page), g) for g in range(pg)]
                + [page_spec((1, width, page), g) for g in range(pg)]
                + [page_spec((1, N_HEADS, page), g) for g in range(pg)])
    return pl.pallas_call(
        functools.partial(_fox_sample_kernel, t_new=t_new, n_chunks=n_chunks),
        grid_spec=pltpu.PrefetchScalarGridSpec(
            num_scalar_prefetch=1,
            grid=(n, n_chunks),
            in_specs=in_specs,
            out_specs=pl.BlockSpec((t_new, width), lambda b, j, pt: (b, 0)),
            scratch_shapes=[pltpu.VMEM((rows, 1), F32), pltpu.VMEM((rows, 1), F32),
                            pltpu.VMEM((rows, width), F32), pltpu.VMEM((N_HEADS, 1), F32)]),
        out_shape=jax.ShapeDtypeStruct((n * t_new, width), F32),
        compiler_params=_params("parallel", "arbitrary"),
        name="fox_sample",
    )(page_table, q, kt_new, vt_new, lft_new, *([cache_kt] * pg), *([cache_vt] * pg), *([cache_lft] * pg))


def _conv_tail(z, b_ref, gl_ref, bl_ref, gco_ref):
    z = z + b_ref[...]
    zc = z - jnp.mean(z, axis=-1, keepdims=True)
    y = zc * lax.rsqrt(jnp.mean(zc * zc, axis=-1, keepdims=True) + EPS) * gl_ref[...] + bl_ref[...]
    y = y * _sigmoid(y)
    return _rms(y, gco_ref[...]).astype(BF16)


def _conv_prompt_kernel(u_ref, halo_ref, w_ref, b_ref, gl_ref, bl_ref, gco_ref, o_ref, xs_ref, *, ts, rc):
    i = pl.program_id(1)
    halo = halo_ref[0]
    xs_ref[0:CONV_HALO, :] = jnp.where(i > 0, halo, jnp.zeros_like(halo))
    xs_ref[CONV_HALO:CONV_HALO + ts, :] = u_ref[0]
    off = CONV_HALO - (CONV_K - 1)
    for r in range(ts // rc):
        acc = jnp.zeros((rc, u_ref.shape[2]), F32)
        for j in range(CONV_K):
            acc = acc + w_ref[j:j + 1, :] * xs_ref[r * rc + off + j:r * rc + off + j + rc, :]
        o_ref[0, r * rc:(r + 1) * rc, :] = _conv_tail(acc, b_ref, gl_ref, bl_ref, gco_ref)


def _conv_prompt(u3, w, b, gl, bl, gco, ts):
    nb, s, c = u3.shape
    hb = ts // CONV_HALO
    vec = _full((1, c))
    return pl.pallas_call(
        functools.partial(_conv_prompt_kernel, ts=ts, rc=32),
        grid=(nb, s // ts),
        in_specs=[pl.BlockSpec((1, ts, c), lambda n, i: (n, i, 0)),
                  pl.BlockSpec((1, CONV_HALO, c), lambda n, i: (n, jnp.maximum(i * hb - 1, 0), 0)),
                  _full(w.shape), vec, vec, vec, vec],
        out_specs=pl.BlockSpec((1, ts, c), lambda n, i: (n, i, 0)),
        out_shape=jax.ShapeDtypeStruct((nb, s, c), BF16),
        scratch_shapes=[pltpu.VMEM((CONV_HALO + ts, c), F32)],
        compiler_params=_params("parallel", "parallel"),
        name="conv_prompt",
    )(u3, u3, w, b, gl, bl, gco)


def _conv_sample_kernel(uh_ref, w_ref, b_ref, gl_ref, bl_ref, gco_ref, o_ref, *, t_new):
    acc = jnp.zeros((t_new, uh_ref.shape[2]), F32)
    for j in range(CONV_K):
        acc = acc + w_ref[j:j + 1, :] * uh_ref[0, j:j + t_new, :]
    o_ref[0] = _conv_tail(acc, b_ref, gl_ref, bl_ref, gco_ref)


def _conv_sample(u_hist, w, b, gl, bl, gco, t_new):
    n, hl, c = u_hist.shape
    vec = _full((1, c))
    return pl.pallas_call(
        functools.partial(_conv_sample_kernel, t_new=t_new),
        grid=(n,),
        in_specs=[pl.BlockSpec((1, hl, c), lambda i: (i, 0, 0)), _full(w.shape), vec, vec, vec, vec],
        out_specs=pl.BlockSpec((1, t_new, c), lambda i: (i, 0, 0)),
        out_shape=jax.ShapeDtypeStruct((n, t_new, c), BF16),
        compiler_params=_params("parallel"),
        name="conv_sample",
    )(u_hist, w, b, gl, bl, gco)


def _out_proj_kernel(x_ref, oa_ref, oc_ref, ga_ref, wa_ref, wc_ref, o_ref):
    a = _rms(oa_ref[...], ga_ref[...]).astype(BF16)
    o_ref[...] = x_ref[...] + _dot(a, wa_ref[...]) + _dot(oc_ref[...], wc_ref[...])


def _out_proj(x, o_attn, o_conv, ga, wa, wc, tm):
    m, d = x.shape
    row = lambda w: pl.BlockSpec((tm, w), lambda i: (i, 0))
    return pl.pallas_call(
        _out_proj_kernel,
        grid=(m // tm,),
        in_specs=[row(d), row(o_attn.shape[1]), row(o_conv.shape[1]), _full(ga.shape),
                  _full(wa.shape), _full(wc.shape)],
        out_specs=row(d),
        out_shape=jax.ShapeDtypeStruct((m, d), F32),
        compiler_params=_params("parallel"),
        name="out_proj",
    )(x, o_attn, o_conv, ga, wa, wc)


def _memory_kv_kernel(mem_ref, g_ref, wk_ref, wv_ref, mk_ref, mv_ref):
    h = _rms(mem_ref[...], g_ref[...]).astype(BF16)
    mk_ref[...] = _dot(h, wk_ref[...])
    mv_ref[...] = _dot(h, wv_ref[...])


def _memory_kv(mem, g, wk, wv, tm):
    m, d = mem.shape
    row = lambda w: pl.BlockSpec((tm, w), lambda i: (i, 0))
    return pl.pallas_call(
        _memory_kv_kernel,
        grid=(m // tm,),
        in_specs=[row(d), _full(g.shape), _full(wk.shape), _full(wv.shape)],
        out_specs=[row(X_WIDTH), row(X_WIDTH)],
        out_shape=[jax.ShapeDtypeStruct((m, X_WIDTH), F32)] * 2,
        compiler_params=_params("parallel"),
        name="memory_kv",
    )(mem, g, wk, wv)


def _cross_kernel(x_ref, g_ref, wq_ref, mk_ref, mv_ref, wo_ref, o_ref):
    x = x_ref[...]
    h = _rms(x, g_ref[...]).astype(BF16)
    q = (_dot(h, wq_ref[...]) * (X_HEAD_DIM ** -0.5)).astype(BF16)
    mk = mk_ref[0].astype(BF16)
    mv = mv_ref[0].astype(BF16)
    outs = []
    for hh in range(X_HEADS):
        sl = slice(hh * X_HEAD_DIM, (hh + 1) * X_HEAD_DIM)
        s = _dot_nt(q[:, sl], mk[:, sl])
        p = jnp.exp(s - jnp.max(s, axis=-1, keepdims=True))
        o = _dot(p.astype(BF16), mv[:, sl]) / jnp.sum(p, axis=-1, keepdims=True)
        outs.append(o.astype(BF16))
    o_ref[...] = x + _dot(jnp.concatenate(outs, axis=1), wo_ref[...])


def _cross(x, g, wq, mk, mv, wo, nb, tm):
    m, d = x.shape
    per = (m // nb) // tm
    n_mem = mk.shape[1]
    return pl.pallas_call(
        _cross_kernel,
        grid=(nb, per),
        in_specs=[pl.BlockSpec((tm, d), lambda b, i: (b * per + i, 0)), _full(g.shape), _full(wq.shape),
                  pl.BlockSpec((1, n_mem, X_WIDTH), lambda b, i: (b, 0, 0)),
                  pl.BlockSpec((1, n_mem, X_WIDTH), lambda b, i: (b, 0, 0)), _full(wo.shape)],
        out_specs=pl.BlockSpec((tm, d), lambda b, i: (b * per + i, 0)),
        out_shape=jax.ShapeDtypeStruct((m, d), F32),
        compiler_params=_params("parallel", "parallel"),
        name="cross_attn",
    )(x, g, wq, mk, mv, wo)


def _ffn_kernel(x_ref, g_ref, wgu_ref, wd_ref, o_ref, acc_ref, *, tf):
    x = x_ref[...]
    h = _rms(x, g_ref[...]).astype(BF16)
    f = wd_ref.shape[0]
    acc_ref[...] = x
    for c in range(f // tf):
        gt = _dot(h, wgu_ref[:, c * tf:(c + 1) * tf])
        up = _dot(h, wgu_ref[:, f + c * tf:f + (c + 1) * tf])
        act = (gt * _sigmoid(gt) * up).astype(BF16)
        acc_ref[...] += _dot(act, wd_ref[c * tf:(c + 1) * tf, :])
    o_ref[...] = acc_ref[...]


def _ffn(x, g, wgu, wd, tm, tf):
    m, d = x.shape
    row = pl.BlockSpec((tm, d), lambda i: (i, 0))
    resident = lambda shape: pl.BlockSpec(shape, lambda i: (0, 0), pipeline_mode=pl.Buffered(1))
    return pl.pallas_call(
        functools.partial(_ffn_kernel, tf=tf),
        grid=(m // tm,),
        in_specs=[row, _full(g.shape), resident(wgu.shape), resident(wd.shape)],
        out_specs=row,
        out_shape=jax.ShapeDtypeStruct((m, d), F32),
        scratch_shapes=[pltpu.VMEM((tm, d), F32)],
        compiler_params=_params("parallel"),
        name="ffn_dense",
    )(x, g, wgu, wd)


def _router_kernel(x_ref, g_ref, wr_ref, br_ref, gates_ref):
    h = _rms(x_ref[...], g_ref[...])
    logits = jnp.dot(h, wr_ref[...], precision=HIGHEST, preferred_element_type=F32) + br_ref[...]
    lane = lax.broadcasted_iota(jnp.int32, logits.shape, 1)
    logits = jnp.where(lane < N_EXPERTS, logits, NEG)
    p = jnp.exp(logits - jnp.max(logits, axis=-1, keepdims=True))
    p = p / jnp.sum(p, axis=-1, keepdims=True)
    v1 = jnp.max(p, axis=-1, keepdims=True)
    i1 = jnp.min(jnp.where(p == v1, lane, LANES), axis=-1, keepdims=True)
    rest = jnp.where(lane == i1, -1.0, p)
    v2 = jnp.max(rest, axis=-1, keepdims=True)
    i2 = jnp.min(jnp.where(rest == v2, lane, LANES), axis=-1, keepdims=True)
    tot = v1 + v2
    gates_ref[...] = jnp.where(lane == i1, v1 / tot, jnp.where(lane == i2, v2 / tot, 0.0))


def _router(x, g, wr, br, tm):
    m, d = x.shape
    return pl.pallas_call(
        _router_kernel,
        grid=(m // tm,),
        in_specs=[pl.BlockSpec((tm, d), lambda i: (i, 0)), _full(g.shape), _full(wr.shape), _full(br.shape)],
        out_specs=pl.BlockSpec((tm, LANES), lambda i: (i, 0)),
        out_shape=jax.ShapeDtypeStruct((m, LANES), F32),
        compiler_params=_params("parallel"),
        name="moe_router",
    )(x, g, wr, br)


def _moe_kernel(x_ref, g_ref, gates_ref, wg_ref, wu_ref, wd_ref, gfin_ref, o_ref, h_scr, acc_scr, *, final):
    e = pl.program_id(1)
    f = pl.program_id(2)

    @pl.when((e == 0) & (f == 0))
    def _():
        x = x_ref[...]
        h_scr[...] = _rms(x, g_ref[...]).astype(BF16)
        acc_scr[...] = x

    h = h_scr[...]
    gt = _dot(h, wg_ref[0])
    up = _dot(h, wu_ref[0])
    act = (gt * _sigmoid(gt) * up).astype(BF16)
    lane = lax.broadcasted_iota(jnp.int32, gates_ref.shape, 1)
    gate = jnp.sum(jnp.where(lane == e, gates_ref[...], 0.0), axis=-1, keepdims=True)
    acc_scr[...] += gate * _dot(act, wd_ref[0])

    @pl.when((e == pl.num_programs(1) - 1) & (f == pl.num_programs(2) - 1))
    def _():
        y = acc_scr[...]
        o_ref[...] = _rms(y, gfin_ref[...]) if final else y


def _moe(x, g, gates, wgu, wd, gfin, final, tm, tf):
    m, d = x.shape
    n_e, f_all, _ = wd.shape
    nf = f_all // tf
    row = lambda w: pl.BlockSpec((tm, w), lambda i, e, f: (i, 0))
    vec = pl.BlockSpec((1, d), lambda i, e, f: (0, 0))
    return pl.pallas_call(
        functools.partial(_moe_kernel, final=final),
        grid=(m // tm, n_e, nf),
        in_specs=[row(d), vec, row(LANES),
                  pl.BlockSpec((1, d, tf), lambda i, e, f: (e, 0, f)),
                  pl.BlockSpec((1, d, tf), lambda i, e, f: (e, 0, nf + f)),
                  pl.BlockSpec((1, tf, d), lambda i, e, f: (e, f, 0)), vec],
        out_specs=row(d),
        out_shape=jax.ShapeDtypeStruct((m, d), F32),
        scratch_shapes=[pltpu.VMEM((tm, d), BF16), pltpu.VMEM((tm, d), F32)],
        compiler_params=_params("parallel", "arbitrary", "arbitrary"),
        name="moe_experts",
    )(x, g, gates, wgu, wgu, wd, gfin)


def _final_norm_kernel(x_ref, g_ref, o_ref):
    o_ref[...] = _rms(x_ref[...], g_ref[...])


def _final_norm(x, g, tm):
    m, d = x.shape
    row = pl.BlockSpec((tm, d), lambda i: (i, 0))
    return pl.pallas_call(
        _final_norm_kernel, grid=(m // tm,), in_specs=[row, _full(g.shape)], out_specs=row,
        out_shape=jax.ShapeDtypeStruct((m, d), F32), compiler_params=_params("parallel"),
        name="final_norm",
    )(x, g)


def _row_tile(m, want):
    t = min(m, want)
    assert m % t == 0, (m, t)
    return t


def kernel(x_prompt, x_sample, cache_k, cache_v, cache_logf, cache_conv, cache_mem_k, cache_mem_v, page_table, mem_prompt, g_mix, w_in, b_fgate, w_dw, b_dw, g_cln, b_cln, g_attn_out, g_conv_out, w_out, g_cross, g_mem, w_cq, w_ckv, w_co, g_ffn, w_ff_gu, w_ff_down, w_router, b_router, w_e_gu, w_e_down, g_final):
    nb, seq, d = x_prompt.shape
    nd, t_new, _ = x_sample.shape
    depth = w_in.shape[0]
    a = ATTN_WIDTH
    cw = d - a
    n_mem = mem_prompt.shape[1]
    n_phys, page = cache_k.shape[1], cache_k.shape[2]
    hist = CONV_K - 1
    vec = lambda v: v.reshape(1, -1).astype(F32)
    pad_lanes = lambda v, fill=0.0: jnp.pad(v, [(0, 0)] * (v.ndim - 1) + [(0, LANES - v.shape[-1])],
                                            constant_values=fill)

    xs = [x_prompt.reshape(nb * seq, d), x_sample.reshape(nd * t_new, d)]
    mem2 = mem_prompt.reshape(nb * n_mem, d)
    cache_kt = cache_k.transpose(0, 1, 3, 4, 2).reshape(depth * n_phys, a, page)
    cache_vt = cache_v.transpose(0, 1, 3, 4, 2).reshape(depth * n_phys, a, page)
    cache_lft = cache_logf.astype(F32).transpose(0, 1, 3, 2).reshape(depth * n_phys, N_HEADS, page)
    outs = {k: [] for k in ("kp", "vp", "lfp", "cvp", "mkp", "mvp", "ks", "vs", "lfs", "cvs")}

    for l in range(depth):
        wqkv = w_in[l][:, :3 * a].astype(BF16)
        wf = pad_lanes(w_in[l][:, 3 * a:3 * a + N_HEADS]).astype(BF16)
        wglu = w_in[l][:, 3 * a + N_HEADS:].astype(BF16)
        bf = pad_lanes(vec(b_fgate[l]))
        wa = w_out[l][:a].astype(BF16)
        wc = w_out[l][a:].astype(BF16)
        wq = w_cq[l].astype(BF16)
        wo = w_co[l].astype(BF16)
        wk_mem = w_ckv[l][:, :X_WIDTH].astype(BF16)
        wv_mem = w_ckv[l][:, X_WIDTH:].astype(BF16)
        conv_args = (w_dw[l].astype(F32), vec(b_dw[l]), vec(g_cln[l]), vec(b_cln[l]), vec(g_conv_out[l]))
        last = l == depth - 1

        mk_p, mv_p = _memory_kv(mem2, vec(g_mem[l]), wk_mem, wv_mem, _row_tile(nb * n_mem, 512))
        outs["mkp"].append(mk_p.reshape(nb, n_mem, X_HEADS, X_HEAD_DIM))
        outs["mvp"].append(mv_p.reshape(nb, n_mem, X_HEADS, X_HEAD_DIM))
        mems = [(mk_p.reshape(nb, n_mem, X_WIDTH), mv_p.reshape(nb, n_mem, X_WIDTH)),
                (cache_mem_k[l].reshape(nd, n_mem, X_WIDTH), cache_mem_v[l].reshape(nd, n_mem, X_WIDTH))]

        for grp in range(2):
            x = xs[grp]
            m = x.shape[0]
            tm = _row_tile(m, 512)
            q, k, v, kb, vb, lf_pad, u = _proj_in(x, vec(g_mix[l]), wqkv, wf, wglu, bf, tm)
            lf = lf_pad[:, :N_HEADS]
            if grp == 0:
                c = _cumsum_logf(lf_pad, nb, seq)
                o_attn = _fox_prompt(q, kb, vb, c, nb, seq, _row_tile(seq, 256))
                u3 = u.reshape(nb, seq, cw)
                o_conv = _conv_prompt(u3, *conv_args, _row_tile(seq, 256)).reshape(m, cw)
                outs["kp"].append(k.reshape(nb, seq, N_HEADS, HEAD_DIM))
                outs["vp"].append(v.reshape(nb, seq, N_HEADS, HEAD_DIM))
                outs["lfp"].append(lf.reshape(nb, seq, N_HEADS))
                outs["cvp"].append(u3[:, seq - hist:])
                n_grp, t_grp = nb, seq
            else:
                new_t = lambda z: jnp.pad(z.reshape(nd, t_new, -1).transpose(0, 2, 1),
                                          ((0, 0), (0, 0), (0, page - t_new)))
                o_attn = _fox_sample(page_table, l * n_phys, q, new_t(k).reshape(nd * a, page),
                                     new_t(v).reshape(nd * a, page), new_t(lf),
                                     cache_kt, cache_vt, cache_lft, t_new)
                u_hist = jnp.concatenate([cache_conv[l].astype(F32), u.reshape(nd, t_new, cw)], axis=1)
                o_conv = _conv_sample(u_hist, *conv_args, t_new).reshape(m, cw)
                outs["ks"].append(k.reshape(nd, t_new, N_HEADS, HEAD_DIM))
                outs["vs"].append(v.reshape(nd, t_new, N_HEADS, HEAD_DIM))
                outs["lfs"].append(lf.reshape(nd, t_new, N_HEADS))
                outs["cvs"].append(u_hist[:, -hist:])
                n_grp, t_grp = nd, t_new
            x = _out_proj(x, o_attn, o_conv, vec(g_attn_out[l]), wa, wc, tm)
            x = _cross(x, vec(g_cross[l]), wq, mems[grp][0], mems[grp][1], wo, n_grp, _row_tile(t_grp, 512))
            if l % 2 == 0:
                x = _ffn(x, vec(g_ffn[l]), w_ff_gu[l // 2].astype(BF16), w_ff_down[l // 2].astype(BF16), tm, 256)
                if last:
                    x = _final_norm(x, vec(g_final), tm)
            else:
                e = l // 2
                gates = _router(x, vec(g_ffn[l]), pad_lanes(w_router[e].astype(F32)),
                                pad_lanes(vec(b_router[e])), tm)
                x = _moe(x, vec(g_ffn[l]), gates, w_e_gu[e].astype(BF16), w_e_down[e].astype(BF16),
                         vec(g_final), last, tm, 512)
            xs[grp] = x

    st = lambda key: jnp.stack(outs[key])
    return (xs[0].reshape(nb, seq, d), xs[1].reshape(nd, t_new, d),
            st("kp"), st("vp"), st("lfp"), st("cvp"), st("mkp"), st("mvp"),
            st("ks"), st("vs"), st("lfs"), st("cvs"))
```

```python
import functools

import jax
import jax.numpy as jnp
from jax import lax
from jax.experimental import pallas as pl
from jax.experimental.pallas import tpu as pltpu

EPS = 1e-6
N_HEADS = 8
HEAD_DIM = 64
ATTN_WIDTH = N_HEADS * HEAD_DIM
CONV_K = 31
CONV_HALO = 32
X_HEADS = 4
X_HEAD_DIM = 128
X_WIDTH = X_HEADS * X_HEAD_DIM
N_EXPERTS = 8
LANES = 128
SUBLANES = 8
NEG = -1e30
VMEM_LIMIT_BYTES = 56 * 1024 * 1024
PAGES_PER_STEP = 16
MOE_TILE = 512
MOE_SPARSE_MIN_TOKENS = 2048

F32 = jnp.float32
BF16 = jnp.bfloat16
HIGHEST = lax.Precision.HIGHEST


def _params(*sem):
    return pltpu.CompilerParams(dimension_semantics=sem, vmem_limit_bytes=VMEM_LIMIT_BYTES)


def _rms(x, g):
    return x * lax.rsqrt(jnp.mean(x * x, axis=-1, keepdims=True) + EPS) * g


def _sigmoid(x):
    return 1.0 / (1.0 + jnp.exp(-x))


def _dot(a, b):
    return jnp.dot(a, b, preferred_element_type=F32)


def _dot_nt(a, b):
    return lax.dot_general(a, b, (((1,), (1,)), ((), ())), preferred_element_type=F32)


def _full(shape):
    nd = len(shape)
    return pl.BlockSpec(shape, lambda *_: (0,) * nd)


def _proj_in_kernel(x_ref, g_ref, wqkv_ref, wf_ref, wglu_ref, bf_ref,
                    q_ref, k_ref, v_ref, kb_ref, vb_ref, lf_ref, u_ref):
    a = ATTN_WIDTH
    h = _rms(x_ref[...], g_ref[...]).astype(BF16)
    qkv = _dot(h, wqkv_ref[...])
    q_ref[...] = (qkv[:, :a] * (HEAD_DIM ** -0.5)).astype(BF16)
    k = qkv[:, a:2 * a]
    v = qkv[:, 2 * a:]
    k_ref[...] = k
    v_ref[...] = v
    kb_ref[...] = k.astype(BF16)
    vb_ref[...] = v.astype(BF16)
    z = _dot(h, wf_ref[...]) + bf_ref[...]
    lf_ref[...] = jnp.minimum(z, 0.0) - jnp.log1p(jnp.exp(-jnp.abs(z)))
    glu = _dot(h, wglu_ref[...])
    c = glu.shape[1] // 2
    u_ref[...] = glu[:, :c] * _sigmoid(glu[:, c:])


def _proj_in(x, g, wqkv, wf, wglu, bf, tm):
    m, d = x.shape
    a = ATTN_WIDTH
    c = wglu.shape[1] // 2
    row = lambda w: pl.BlockSpec((tm, w), lambda i: (i, 0))
    return pl.pallas_call(
        _proj_in_kernel,
        grid=(m // tm,),
        in_specs=[row(d), _full((1, d)), _full(wqkv.shape), _full(wf.shape), _full(wglu.shape),
                  _full((1, LANES))],
        out_specs=[row(a), row(a), row(a), row(a), row(a), row(LANES), row(c)],
        out_shape=[jax.ShapeDtypeStruct((m, a), BF16), jax.ShapeDtypeStruct((m, a), F32),
                   jax.ShapeDtypeStruct((m, a), F32), jax.ShapeDtypeStruct((m, a), BF16),
                   jax.ShapeDtypeStruct((m, a), BF16), jax.ShapeDtypeStruct((m, LANES), F32),
                   jax.ShapeDtypeStruct((m, c), F32)],
        compiler_params=_params("parallel"),
        name="proj_in",
    )(x, g, wqkv, wf, wglu, bf)


def _cumsum_kernel(lf_ref, c_ref):
    s = lf_ref.shape[0]
    lft = lf_ref[...].T[:N_HEADS, :]
    r = lax.broadcasted_iota(jnp.int32, (LANES, LANES), 0)
    cidx = lax.broadcasted_iota(jnp.int32, (LANES, LANES), 1)
    tri = (r <= cidx).astype(F32)
    carry = jnp.zeros((N_HEADS, 1), F32)
    for b in range(s // LANES):
        cb = jnp.dot(lft[:, b * LANES:(b + 1) * LANES], tri, precision=HIGHEST,
                     preferred_element_type=F32) + carry
        c_ref[0, :, b * LANES:(b + 1) * LANES] = cb
        carry = cb[:, LANES - 1:LANES]


def _cumsum_logf(lf_pad, nb, s):
    return pl.pallas_call(
        _cumsum_kernel,
        grid=(nb,),
        in_specs=[pl.BlockSpec((s, LANES), lambda b: (b, 0))],
        out_specs=pl.BlockSpec((1, N_HEADS, s), lambda b: (b, 0, 0)),
        out_shape=jax.ShapeDtypeStruct((nb, N_HEADS, s), F32),
        compiler_params=_params("parallel"),
        name="cumsum_logf",
    )(lf_pad)


def _fox_prompt_kernel(q_ref, k_ref, v_ref, c_ref, o_ref, m_scr, l_scr, acc_scr, *, t):
    i = pl.program_id(2)
    lane = lax.broadcasted_iota(jnp.int32, (1, LANES), 1)
    q = q_ref[...]
    zero = jnp.zeros_like(q)
    qh = (jnp.where(lane < HEAD_DIM, q, zero), jnp.where(lane >= HEAD_DIM, q, zero))
    m_scr[...] = jnp.full(m_scr.shape, NEG, F32)
    l_scr[...] = jnp.zeros(l_scr.shape, F32)
    acc_scr[...] = jnp.zeros(acc_scr.shape, F32)
    row = lax.broadcasted_iota(jnp.int32, (t, t), 0)
    col = lax.broadcasted_iota(jnp.int32, (t, t), 1)

    def step(j, masked):
        start = pl.multiple_of(j * t, t)
        kb = k_ref[pl.ds(start, t), :]
        vb = v_ref[pl.ds(start, t), :]
        cs = c_ref[:, j]
        for hh in range(2):
            s = _dot_nt(qh[hh], kb) - cs[hh]
            if masked:
                s = jnp.where(col <= row, s, NEG)
            m_prev = m_scr[hh]
            m_new = jnp.maximum(m_prev, jnp.max(s, axis=-1, keepdims=True))
            alpha = jnp.exp(m_prev - m_new)
            p = jnp.exp(s - jnp.concatenate([m_new] * (t // LANES), axis=1))
            l_scr[hh] = alpha * l_scr[hh] + jnp.sum(p, axis=-1, keepdims=True)
            acc_scr[hh] = alpha * acc_scr[hh] + _dot(p.astype(BF16), vb)
            m_scr[hh] = m_new

    def body(j, carry):
        step(j, False)
        return carry

    lax.fori_loop(0, i, body, 0)
    step(i, True)
    o_ref[...] = jnp.where(lane < HEAD_DIM, acc_scr[0] / l_scr[0], acc_scr[1] / l_scr[1])


def _fox_prompt(q, kb, vb, c, nb, s, t):
    m = q.shape[0]
    nt = s // t
    c4 = c.reshape(nb * N_HEADS, nt, 1, t)
    pairs = N_HEADS // 2
    return pl.pallas_call(
        functools.partial(_fox_prompt_kernel, t=t),
        grid=(nb, pairs, nt),
        in_specs=[pl.BlockSpec((t, LANES), lambda b, p, i: (b * nt + i, p)),
                  pl.BlockSpec((s, LANES), lambda b, p, i: (b, p)),
                  pl.BlockSpec((s, LANES), lambda b, p, i: (b, p)),
                  pl.BlockSpec((2, nt, 1, t), lambda b, p, i: (b * pairs + p, 0, 0, 0))],
        out_specs=pl.BlockSpec((t, LANES), lambda b, p, i: (b * nt + i, p)),
        out_shape=jax.ShapeDtypeStruct((m, ATTN_WIDTH), F32),
        scratch_shapes=[pltpu.VMEM((2, t, LANES), F32), pltpu.VMEM((2, t, LANES), F32),
                        pltpu.VMEM((2, t, LANES), F32)],
        compiler_params=_params("parallel", "parallel", "arbitrary"),
        name="fox_prompt",
    )(q, kb, vb, c4)


def _fox_sample_kernel(pt_ref, q_ref, kn_ref, vn_ref, lfn_ref, *rest, t_new, n_chunks):
    pg = PAGES_PER_STEP
    k_refs, v_refs, lf_refs = rest[:pg], rest[pg:2 * pg], rest[2 * pg:3 * pg]
    o_ref, m_scr, l_scr, acc_scr, carry_scr = rest[3 * pg:]
    del pt_ref
    j = pl.program_id(1)
    rows = N_HEADS * t_new
    width = ATTN_WIDTH
    page = lf_refs[0].shape[2]

    @pl.when(j == 0)
    def _():
        m_scr[...] = jnp.full(m_scr.shape, NEG, F32)
        l_scr[...] = jnp.zeros(l_scr.shape, F32)
        acc_scr[...] = jnp.zeros(acc_scr.shape, F32)
        carry_scr[...] = jnp.zeros(carry_scr.shape, F32)

    q = q_ref[...]
    q_rows = jnp.concatenate([q] * N_HEADS, axis=0)
    rr = lax.broadcasted_iota(jnp.int32, (rows, width), 0)
    ll = lax.broadcasted_iota(jnp.int32, (rows, width), 1)
    head_mask = (rr // t_new) == (ll // HEAD_DIM)
    q_rows = jnp.where(head_mask, q_rows, jnp.zeros_like(q_rows))

    def per_query_rows(x):
        return jnp.concatenate([jnp.broadcast_to(x[h:h + 1], (t_new, x.shape[1])) for h in range(N_HEADS)],
                               axis=0)

    def attend(kts, vts, biases):
        s = jnp.concatenate([_dot(q_rows, kt) + b for kt, b in zip(kts, biases)], axis=1)
        m_prev = m_scr[...]
        m_new = jnp.maximum(m_prev, jnp.max(s, axis=-1, keepdims=True))
        alpha = jnp.exp(m_prev - m_new)
        p = jnp.exp(s - m_new)
        l_scr[...] = alpha * l_scr[...] + jnp.sum(p, axis=-1, keepdims=True)
        p = p.astype(BF16)
        pv = _dot_nt(p[:, :page], vts[0])
        for g in range(1, len(vts)):
            pv = pv + _dot_nt(p[:, g * page:(g + 1) * page], vts[g])
        acc_scr[...] = alpha * acc_scr[...] + pv
        m_scr[...] = m_new

    lf = jnp.concatenate([lf_refs[g][0] for g in range(pg)], axis=0)
    lane_all = lax.broadcasted_iota(jnp.int32, lf.shape, 1)
    suffix = lf
    shift = 1
    while shift < page:
        nxt = pltpu.roll(suffix, page - shift, axis=1)
        suffix = suffix + jnp.where(lane_all + shift < page, nxt, 0.0)
        shift *= 2
    within = suffix - lf
    totals = jnp.sum(lf, axis=-1, keepdims=True)
    later = carry_scr[...]
    biases = [None] * pg
    for g in reversed(range(pg)):
        biases[g] = per_query_rows(within[g * N_HEADS:(g + 1) * N_HEADS] + later)
        later = later + totals[g * N_HEADS:(g + 1) * N_HEADS]
    carry_scr[...] = later
    attend([k_refs[g][0].astype(BF16) for g in range(pg)], [v_refs[g][0].astype(BF16) for g in range(pg)], biases)
    lane = lax.broadcasted_iota(jnp.int32, (N_HEADS, page), 1)

    @pl.when(j == n_chunks - 1)
    def _():
        c_new = lfn_ref[0]
        shift = 1
        while shift < t_new:
            c_new = c_new + jnp.where(lane >= shift, pltpu.roll(c_new, shift, axis=1), 0.0)
            shift *= 2
        br = lax.broadcasted_iota(jnp.int32, (rows, page), 0)
        bc = lax.broadcasted_iota(jnp.int32, (rows, page), 1)
        bias = jnp.where(bc <= (br % t_new), -per_query_rows(c_new), NEG)
        attend([kn_ref[...].astype(BF16)], [vn_ref[...].astype(BF16)], [bias])
        o = jnp.where(head_mask, acc_scr[...] / l_scr[...], 0.0)
        out = o[0:t_new]
        for h in range(1, N_HEADS):
            out = out + o[h * t_new:(h + 1) * t_new]
        o_ref[...] = out


def _fox_sample(page_table, base, q, kt_new, vt_new, lft_new, cache_kt, cache_vt, cache_lft, t_new):
    n, n_pages = page_table.shape
    pg = PAGES_PER_STEP
    n_chunks = n_pages // pg
    assert n_pages % pg == 0
    width, page = cache_kt.shape[1], cache_kt.shape[2]
    rows = N_HEADS * t_new

    def page_spec(shape, g):
        return pl.BlockSpec(shape, lambda b, j, pt: (base + pt[b, (n_chunks - 1 - j) * pg + g], 0, 0))

    in_specs = ([pl.BlockSpec((t_new, width), lambda b, j, pt: (b, 0)),
                 pl.BlockSpec((width, page), lambda b, j, pt: (b, 0)),
                 pl.BlockSpec((width, page), lambda b, j, pt: (b, 0)),
                 pl.BlockSpec((1, N_HEADS, page), lambda b, j, pt: (b, 0, 0))]
                + [page_spec((1, width, page), g) for g in range(pg)]
                + [page_spec((1, width, page), g) for g in range(pg)]
                + [page_spec((1, N_HEADS, page), g) for g in range(pg)])
    return pl.pallas_call(
        functools.partial(_fox_sample_kernel, t_new=t_new, n_chunks=n_chunks),
        grid_spec=pltpu.PrefetchScalarGridSpec(
            num_scalar_prefetch=1,
            grid=(n, n_chunks),
            in_specs=in_specs,
            out_specs=pl.BlockSpec((t_new, width), lambda b, j, pt: (b, 0)),
            scratch_shapes=[pltpu.VMEM((rows, 1), F32), pltpu.VMEM((rows, 1), F32),
                            pltpu.VMEM((rows, width), F32), pltpu.VMEM((N_HEADS, 1), F32)]),
        out_shape=jax.ShapeDtypeStruct((n * t_new, width), F32),
        compiler_params=_params("parallel", "arbitrary"),
        name="fox_sample",
    )(page_table, q, kt_new, vt_new, lft_new, *([cache_kt] * pg), *([cache_vt] * pg), *([cache_lft] * pg))


def _conv_tail(z, b_ref, gl_ref, bl_ref, gco_ref):
    z = z + b_ref[...]
    zc = z - jnp.mean(z, axis=-1, keepdims=True)
    y = zc * lax.rsqrt(jnp.mean(zc * zc, axis=-1, keepdims=True) + EPS) * gl_ref[...] + bl_ref[...]
    y = y * _sigmoid(y)
    return _rms(y, gco_ref[...]).astype(BF16)


def _conv_prompt_kernel(u_ref, halo_ref, w_ref, b_ref, gl_ref, bl_ref, gco_ref, o_ref, xs_ref, sh_ref, *, ts, rc):
    i = pl.program_id(1)
    halo = halo_ref[0]
    xs_ref[0:CONV_HALO, :] = jnp.where(i > 0, halo, jnp.zeros_like(halo))
    xs_ref[CONV_HALO:CONV_HALO + ts, :] = u_ref[0]
    n_sh = sh_ref.shape[1]
    for r in range(1, SUBLANES):
        sh_ref[r - 1] = xs_ref[r:r + n_sh, :]
    off = CONV_HALO - (CONV_K - 1)
    width = u_ref.shape[2]
    for c in range(ts // rc):
        acc = jnp.zeros((rc // SUBLANES, SUBLANES, width), F32)
        for j in range(CONV_K):
            q, r = divmod(off + j, SUBLANES)
            lo = c * rc + q * SUBLANES
            rows = xs_ref[lo:lo + rc, :] if r == 0 else sh_ref[r - 1, lo:lo + rc, :]
            acc = acc + w_ref[j][None] * rows.reshape(rc // SUBLANES, SUBLANES, width)
        o_ref[0, c * rc:(c + 1) * rc, :] = _conv_tail(acc.reshape(rc, width), b_ref, gl_ref, bl_ref, gco_ref)


def _conv_prompt(u3, w, b, gl, bl, gco, ts):
    nb, s, c = u3.shape
    hb = ts // CONV_HALO
    vec = _full((1, c))
    w = jnp.broadcast_to(w[:, None, :], (w.shape[0], SUBLANES, c))
    return pl.pallas_call(
        functools.partial(_conv_prompt_kernel, ts=ts, rc=32),
        grid=(nb, s // ts),
        in_specs=[pl.BlockSpec((1, ts, c), lambda n, i: (n, i, 0)),
                  pl.BlockSpec((1, CONV_HALO, c), lambda n, i: (n, jnp.maximum(i * hb - 1, 0), 0)),
                  _full(w.shape), vec, vec, vec, vec],
        out_specs=pl.BlockSpec((1, ts, c), lambda n, i: (n, i, 0)),
        out_shape=jax.ShapeDtypeStruct((nb, s, c), BF16),
        scratch_shapes=[pltpu.VMEM((CONV_HALO + ts, c), F32),
                        pltpu.VMEM((SUBLANES - 1, CONV_HALO + ts - SUBLANES, c), F32)],
        compiler_params=_params("parallel", "parallel"),
        name="conv_prompt",
    )(u3, u3, w, b, gl, bl, gco)


def _conv_sample_kernel(uh_ref, w_ref, b_ref, gl_ref, bl_ref, gco_ref, o_ref, *, t_new):
    acc = jnp.zeros((t_new, uh_ref.shape[2]), F32)
    for j in range(CONV_K):
        acc = acc + w_ref[j:j + 1, :] * uh_ref[0, j:j + t_new, :]
    o_ref[0] = _conv_tail(acc, b_ref, gl_ref, bl_ref, gco_ref)


def _conv_sample(u_hist, w, b, gl, bl, gco, t_new):
    n, hl, c = u_hist.shape
    vec = _full((1, c))
    return pl.pallas_call(
        functools.partial(_conv_sample_kernel, t_new=t_new),
        grid=(n,),
        in_specs=[pl.BlockSpec((1, hl, c), lambda i: (i, 0, 0)), _full(w.shape), vec, vec, vec, vec],
        out_specs=pl.BlockSpec((1, t_new, c), lambda i: (i, 0, 0)),
        out_shape=jax.ShapeDtypeStruct((n, t_new, c), BF16),
        compiler_params=_params("parallel"),
        name="conv_sample",
    )(u_hist, w, b, gl, bl, gco)


def _out_proj_kernel(x_ref, oa_ref, oc_ref, ga_ref, wa_ref, wc_ref, o_ref):
    a = _rms(oa_ref[...], ga_ref[...]).astype(BF16)
    o_ref[...] = x_ref[...] + _dot(a, wa_ref[...]) + _dot(oc_ref[...], wc_ref[...])


def _out_proj(x, o_attn, o_conv, ga, wa, wc, tm):
    m, d = x.shape
    row = lambda w: pl.BlockSpec((tm, w), lambda i: (i, 0))
    return pl.pallas_call(
        _out_proj_kernel,
        grid=(m // tm,),
        in_specs=[row(d), row(o_attn.shape[1]), row(o_conv.shape[1]), _full(ga.shape),
                  _full(wa.shape), _full(wc.shape)],
        out_specs=row(d),
        out_shape=jax.ShapeDtypeStruct((m, d), F32),
        compiler_params=_params("parallel"),
        name="out_proj",
    )(x, o_attn, o_conv, ga, wa, wc)


def _memory_kv_kernel(mem_ref, g_ref, wk_ref, wv_ref, mk_ref, mv_ref):
    h = _rms(mem_ref[...], g_ref[...]).astype(BF16)
    mk_ref[...] = _dot(h, wk_ref[...])
    mv_ref[...] = _dot(h, wv_ref[...])


def _memory_kv(mem, g, wk, wv, tm):
    m, d = mem.shape
    row = lambda w: pl.BlockSpec((tm, w), lambda i: (i, 0))
    return pl.pallas_call(
        _memory_kv_kernel,
        grid=(m // tm,),
        in_specs=[row(d), _full(g.shape), _full(wk.shape), _full(wv.shape)],
        out_specs=[row(X_WIDTH), row(X_WIDTH)],
        out_shape=[jax.ShapeDtypeStruct((m, X_WIDTH), F32)] * 2,
        compiler_params=_params("parallel"),
        name="memory_kv",
    )(mem, g, wk, wv)


def _cross_kernel(x_ref, g_ref, wq_ref, mk_ref, mv_ref, wo_ref, o_ref):
    x = x_ref[...]
    h = _rms(x, g_ref[...]).astype(BF16)
    q = (_dot(h, wq_ref[...]) * (X_HEAD_DIM ** -0.5)).astype(BF16)
    mk = mk_ref[0].astype(BF16)
    mv = mv_ref[0].astype(BF16)
    outs = []
    for hh in range(X_HEADS):
        sl = slice(hh * X_HEAD_DIM, (hh + 1) * X_HEAD_DIM)
        s = _dot_nt(q[:, sl], mk[:, sl])
        p = jnp.exp(s - jnp.max(s, axis=-1, keepdims=True))
        o = _dot(p.astype(BF16), mv[:, sl]) / jnp.sum(p, axis=-1, keepdims=True)
        outs.append(o.astype(BF16))
    o_ref[...] = x + _dot(jnp.concatenate(outs, axis=1), wo_ref[...])


def _cross(x, g, wq, mk, mv, wo, nb, tm):
    m, d = x.shape
    per = (m // nb) // tm
    n_mem = mk.shape[1]
    return pl.pallas_call(
        _cross_kernel,
        grid=(nb, per),
        in_specs=[pl.BlockSpec((tm, d), lambda b, i: (b * per + i, 0)), _full(g.shape), _full(wq.shape),
                  pl.BlockSpec((1, n_mem, X_WIDTH), lambda b, i: (b, 0, 0)),
                  pl.BlockSpec((1, n_mem, X_WIDTH), lambda b, i: (b, 0, 0)), _full(wo.shape)],
        out_specs=pl.BlockSpec((tm, d), lambda b, i: (b * per + i, 0)),
        out_shape=jax.ShapeDtypeStruct((m, d), F32),
        compiler_params=_params("parallel", "parallel"),
        name="cross_attn",
    )(x, g, wq, mk, mv, wo)


def _ffn_kernel(x_ref, g_ref, wgu_ref, wd_ref, o_ref, acc_ref, *, tf):
    x = x_ref[...]
    h = _rms(x, g_ref[...]).astype(BF16)
    f = wd_ref.shape[0]
    acc_ref[...] = x
    for c in range(f // tf):
        gt = _dot(h, wgu_ref[:, c * tf:(c + 1) * tf])
        up = _dot(h, wgu_ref[:, f + c * tf:f + (c + 1) * tf])
        act = (gt * _sigmoid(gt) * up).astype(BF16)
        acc_ref[...] += _dot(act, wd_ref[c * tf:(c + 1) * tf, :])
    o_ref[...] = acc_ref[...]


def _ffn(x, g, wgu, wd, tm, tf):
    m, d = x.shape
    row = pl.BlockSpec((tm, d), lambda i: (i, 0))
    resident = lambda shape: pl.BlockSpec(shape, lambda i: (0, 0), pipeline_mode=pl.Buffered(1))
    return pl.pallas_call(
        functools.partial(_ffn_kernel, tf=tf),
        grid=(m // tm,),
        in_specs=[row, _full(g.shape), resident(wgu.shape), resident(wd.shape)],
        out_specs=row,
        out_shape=jax.ShapeDtypeStruct((m, d), F32),
        scratch_shapes=[pltpu.VMEM((tm, d), F32)],
        compiler_params=_params("parallel"),
        name="ffn_dense",
    )(x, g, wgu, wd)


def _router_kernel(x_ref, g_ref, wr_ref, br_ref, gates_ref, top_ref):
    h = _rms(x_ref[...], g_ref[...])
    logits = jnp.dot(h, wr_ref[...], precision=HIGHEST, preferred_element_type=F32) + br_ref[...]
    lane = lax.broadcasted_iota(jnp.int32, logits.shape, 1)
    logits = jnp.where(lane < N_EXPERTS, logits, NEG)
    p = jnp.exp(logits - jnp.max(logits, axis=-1, keepdims=True))
    p = p / jnp.sum(p, axis=-1, keepdims=True)
    v1 = jnp.max(p, axis=-1, keepdims=True)
    i1 = jnp.min(jnp.where(p == v1, lane, LANES), axis=-1, keepdims=True)
    rest = jnp.where(lane == i1, -1.0, p)
    v2 = jnp.max(rest, axis=-1, keepdims=True)
    i2 = jnp.min(jnp.where(rest == v2, lane, LANES), axis=-1, keepdims=True)
    tot = v1 + v2
    g1 = v1 / tot
    g2 = v2 / tot
    gates_ref[...] = jnp.where(lane == i1, g1, jnp.where(lane == i2, g2, 0.0))
    top_ref[...] = jnp.where(lane == 0, i1.astype(F32),
                             jnp.where(lane == 1, i2.astype(F32),
                                       jnp.where(lane == 2, g1, jnp.where(lane == 3, g2, 0.0))))


def _router(x, g, wr, br, tm):
    m, d = x.shape
    out = pl.BlockSpec((tm, LANES), lambda i: (i, 0))
    return pl.pallas_call(
        _router_kernel,
        grid=(m // tm,),
        in_specs=[pl.BlockSpec((tm, d), lambda i: (i, 0)), _full(g.shape), _full(wr.shape), _full(br.shape)],
        out_specs=[out, out],
        out_shape=[jax.ShapeDtypeStruct((m, LANES), F32)] * 2,
        compiler_params=_params("parallel"),
        name="moe_router",
    )(x, g, wr, br)


def _copy_rows(src_hbm, row_of, buf, sem, n):
    def issue(r, carry):
        pltpu.make_async_copy(src_hbm.at[pl.ds(row_of(r), 1)], buf.at[pl.ds(r, 1)], sem).start()
        return carry

    lax.fori_loop(0, n, issue, 0, unroll=8)
    pltpu.make_async_copy(src_hbm.at[pl.ds(0, n)], buf, sem).wait()


def _moe_gather_kernel(tok_ref, nvalid_ref, x_hbm, g_ref, o_ref, buf, sem, *, tm):
    i = pl.program_id(0)

    @pl.when(i < nvalid_ref[0])
    def _():
        _copy_rows(x_hbm, lambda r: tok_ref[i * tm + r], buf, sem, tm)
        o_ref[...] = _rms(buf[...], g_ref[...]).astype(BF16)

    @pl.when(i >= nvalid_ref[0])
    def _():
        o_ref[...] = jnp.zeros(o_ref.shape, BF16)


def _moe_gather(slot_token, n_valid, x, g, tm):
    m, d = x.shape
    n_slots = slot_token.shape[0]
    return pl.pallas_call(
        functools.partial(_moe_gather_kernel, tm=tm),
        grid_spec=pltpu.PrefetchScalarGridSpec(
            num_scalar_prefetch=2,
            grid=(n_slots // tm,),
            in_specs=[pl.BlockSpec(memory_space=pl.ANY), pl.BlockSpec((1, d), lambda i, *_: (0, 0))],
            out_specs=pl.BlockSpec((tm, d), lambda i, *_: (i, 0)),
            scratch_shapes=[pltpu.VMEM((tm, d), F32), pltpu.SemaphoreType.DMA(())]),
        out_shape=jax.ShapeDtypeStruct((n_slots, d), BF16),
        compiler_params=_params("arbitrary"),
        name="moe_gather",
    )(slot_token, n_valid, x, g)


def _moe_sparse_kernel(te_ref, nvalid_ref, x_ref, gate_ref, wg_ref, wu_ref, wd_ref, o_ref, acc_scr):
    del te_ref
    i = pl.program_id(0)
    f = pl.program_id(1)
    last = pl.num_programs(1) - 1
    valid = i < nvalid_ref[0]

    @pl.when(f == 0)
    def _():
        acc_scr[...] = jnp.zeros(acc_scr.shape, F32)

    @pl.when(valid)
    def _():
        h = x_ref[...]
        gt = _dot(h, wg_ref[0])
        up = _dot(h, wu_ref[0])
        act = (gt * _sigmoid(gt) * up).astype(BF16)
        acc_scr[...] += _dot(act, wd_ref[0])

    @pl.when(f == last)
    def _():
        o_ref[...] = gate_ref[...] * acc_scr[...]


def _moe_sparse(tile_expert, n_valid, xs, slot_gate, wgu, wd, tm, tf):
    n_slots, d = xs.shape
    f_all = wd.shape[1]
    nf = f_all // tf

    def chunk(i, f, nv):
        return jnp.where(i < nv[0], f, nf - 1)

    return pl.pallas_call(
        _moe_sparse_kernel,
        grid_spec=pltpu.PrefetchScalarGridSpec(
            num_scalar_prefetch=2,
            grid=(n_slots // tm, nf),
            in_specs=[pl.BlockSpec((tm, d), lambda i, f, te, nv: (i, 0)),
                      pl.BlockSpec((tm, 1), lambda i, f, te, nv: (i, 0)),
                      pl.BlockSpec((1, d, tf), lambda i, f, te, nv: (te[i], 0, chunk(i, f, nv))),
                      pl.BlockSpec((1, d, tf), lambda i, f, te, nv: (te[i], 0, nf + chunk(i, f, nv))),
                      pl.BlockSpec((1, tf, d), lambda i, f, te, nv: (te[i], chunk(i, f, nv), 0))],
            out_specs=pl.BlockSpec((tm, d), lambda i, f, te, nv: (i, 0)),
            scratch_shapes=[pltpu.VMEM((tm, d), F32)]),
        out_shape=jax.ShapeDtypeStruct((n_slots, d), F32),
        compiler_params=_params("arbitrary", "arbitrary"),
        name="moe_sparse",
    )(tile_expert, n_valid, xs, slot_gate, wgu, wgu, wd)


def _moe_combine_kernel(slot_ref, x_ref, gfin_ref, ys_hbm, o_ref, buf_a, buf_b, sem_a, sem_b, *, tc, m, final):
    i = pl.program_id(0)
    _copy_rows(ys_hbm, lambda r: slot_ref[i * tc + r], buf_a, sem_a, tc)
    _copy_rows(ys_hbm, lambda r: slot_ref[m + i * tc + r], buf_b, sem_b, tc)
    y = x_ref[...] + buf_a[...] + buf_b[...]
    o_ref[...] = _rms(y, gfin_ref[...]) if final else y


def _moe_combine(token_slot, x, gfin, ys, final, tc):
    m, d = x.shape
    return pl.pallas_call(
        functools.partial(_moe_combine_kernel, tc=tc, m=m, final=final),
        grid_spec=pltpu.PrefetchScalarGridSpec(
            num_scalar_prefetch=1,
            grid=(m // tc,),
            in_specs=[pl.BlockSpec((tc, d), lambda i, *_: (i, 0)), pl.BlockSpec((1, d), lambda i, *_: (0, 0)),
                      pl.BlockSpec(memory_space=pl.ANY)],
            out_specs=pl.BlockSpec((tc, d), lambda i, *_: (i, 0)),
            scratch_shapes=[pltpu.VMEM((tc, d), F32), pltpu.VMEM((tc, d), F32),
                            pltpu.SemaphoreType.DMA(()), pltpu.SemaphoreType.DMA(())]),
        out_shape=jax.ShapeDtypeStruct((m, d), F32),
        compiler_params=_params("arbitrary"),
        name="moe_combine",
    )(token_slot, x, gfin, ys)


def _moe_routing(top, tm):
    m = top.shape[0]
    experts = jnp.concatenate([top[:, 0], top[:, 1]]).astype(jnp.int32)
    gates = jnp.concatenate([top[:, 2], top[:, 3]])
    onehot = (experts[:, None] == jnp.arange(N_EXPERTS, dtype=jnp.int32)[None, :]).astype(jnp.int32)
    rank = jnp.sum((jnp.cumsum(onehot, axis=0) - onehot) * onehot, axis=1)
    counts = jnp.sum(onehot, axis=0)
    padded = ((counts + tm - 1) // tm) * tm
    ends = jnp.cumsum(padded)
    slot = (ends - padded)[experts] + rank
    n_tiles = (2 * m) // tm + N_EXPERTS
    n_slots = n_tiles * tm
    token = jnp.arange(2 * m, dtype=jnp.int32) % m
    slot_token = jnp.zeros((n_slots,), jnp.int32).at[slot].set(token)
    slot_gate = jnp.zeros((n_slots,), F32).at[slot].set(gates)
    tile_start = jnp.arange(n_tiles, dtype=jnp.int32) * tm
    tile_expert = jnp.minimum(jnp.sum((tile_start[:, None] >= ends[None, :]).astype(jnp.int32), axis=1),
                              N_EXPERTS - 1)
    n_valid = (ends[-1] // tm).astype(jnp.int32).reshape(1)
    return slot.astype(jnp.int32), slot_token, slot_gate.reshape(n_slots, 1), tile_expert.astype(jnp.int32), n_valid


def _moe_kernel(x_ref, g_ref, gates_ref, wg_ref, wu_ref, wd_ref, gfin_ref, o_ref, h_scr, acc_scr, *, final):
    e = pl.program_id(1)
    f = pl.program_id(2)

    @pl.when((e == 0) & (f == 0))
    def _():
        x = x_ref[...]
        h_scr[...] = _rms(x, g_ref[...]).astype(BF16)
        acc_scr[...] = x

    h = h_scr[...]
    gt = _dot(h, wg_ref[0])
    up = _dot(h, wu_ref[0])
    act = (gt * _sigmoid(gt) * up).astype(BF16)
    lane = lax.broadcasted_iota(jnp.int32, gates_ref.shape, 1)
    gate = jnp.sum(jnp.where(lane == e, gates_ref[...], 0.0), axis=-1, keepdims=True)
    acc_scr[...] += gate * _dot(act, wd_ref[0])

    @pl.when((e == pl.num_programs(1) - 1) & (f == pl.num_programs(2) - 1))
    def _():
        y = acc_scr[...]
        o_ref[...] = _rms(y, gfin_ref[...]) if final else y


def _moe(x, g, gates, wgu, wd, gfin, final, tm, tf):
    m, d = x.shape
    n_e, f_all, _ = wd.shape
    nf = f_all // tf
    row = lambda w: pl.BlockSpec((tm, w), lambda i, e, f: (i, 0))
    vec = pl.BlockSpec((1, d), lambda i, e, f: (0, 0))
    return pl.pallas_call(
        functools.partial(_moe_kernel, final=final),
        grid=(m // tm, n_e, nf),
        in_specs=[row(d), vec, row(LANES),
                  pl.BlockSpec((1, d, tf), lambda i, e, f: (e, 0, f)),
                  pl.BlockSpec((1, d, tf), lambda i, e, f: (e, 0, nf + f)),
                  pl.BlockSpec((1, tf, d), lambda i, e, f: (e, f, 0)), vec],
        out_specs=row(d),
        out_shape=jax.ShapeDtypeStruct((m, d), F32),
        scratch_shapes=[pltpu.VMEM((tm, d), BF16), pltpu.VMEM((tm, d), F32)],
        compiler_params=_params("parallel", "arbitrary", "arbitrary"),
        name="moe_experts",
    )(x, g, gates, wgu, wgu, wd, gfin)


def _final_norm_kernel(x_ref, g_ref, o_ref):
    o_ref[...] = _rms(x_ref[...], g_ref[...])


def _final_norm(x, g, tm):
    m, d = x.shape
    row = pl.BlockSpec((tm, d), lambda i: (i, 0))
    return pl.pallas_call(
        _final_norm_kernel, grid=(m // tm,), in_specs=[row, _full(g.shape)], out_specs=row,
        out_shape=jax.ShapeDtypeStruct((m, d), F32), compiler_params=_params("parallel"),
        name="final_norm",
    )(x, g)


def _row_tile(m, want):
    t = min(m, want)
    assert m % t == 0, (m, t)
    return t


def kernel(x_prompt, x_sample, cache_k, cache_v, cache_logf, cache_conv, cache_mem_k, cache_mem_v, page_table, mem_prompt, g_mix, w_in, b_fgate, w_dw, b_dw, g_cln, b_cln, g_attn_out, g_conv_out, w_out, g_cross, g_mem, w_cq, w_ckv, w_co, g_ffn, w_ff_gu, w_ff_down, w_router, b_router, w_e_gu, w_e_down, g_final):
    nb, seq, d = x_prompt.shape
    nd, t_new, _ = x_sample.shape
    depth = w_in.shape[0]
    a = ATTN_WIDTH
    cw = d - a
    n_mem = mem_prompt.shape[1]
    n_phys, page = cache_k.shape[1], cache_k.shape[2]
    hist = CONV_K - 1
    vec = lambda v: v.reshape(1, -1).astype(F32)
    pad_lanes = lambda v, fill=0.0: jnp.pad(v, [(0, 0)] * (v.ndim - 1) + [(0, LANES - v.shape[-1])],
                                            constant_values=fill)

    xs = [x_prompt.reshape(nb * seq, d), x_sample.reshape(nd * t_new, d)]
    mem2 = mem_prompt.reshape(nb * n_mem, d)
    cache_kt = cache_k.transpose(0, 1, 3, 4, 2).reshape(depth * n_phys, a, page)
    cache_vt = cache_v.transpose(0, 1, 3, 4, 2).reshape(depth * n_phys, a, page)
    cache_lft = cache_logf.astype(F32).transpose(0, 1, 3, 2).reshape(depth * n_phys, N_HEADS, page)
    outs = {k: [] for k in ("kp", "vp", "lfp", "cvp", "mkp", "mvp", "ks", "vs", "lfs", "cvs")}

    for l in range(depth):
        wqkv = w_in[l][:, :3 * a].astype(BF16)
        wf = pad_lanes(w_in[l][:, 3 * a:3 * a + N_HEADS]).astype(BF16)
        wglu = w_in[l][:, 3 * a + N_HEADS:].astype(BF16)
        bf = pad_lanes(vec(b_fgate[l]))
        wa = w_out[l][:a].astype(BF16)
        wc = w_out[l][a:].astype(BF16)
        wq = w_cq[l].astype(BF16)
        wo = w_co[l].astype(BF16)
        wk_mem = w_ckv[l][:, :X_WIDTH].astype(BF16)
        wv_mem = w_ckv[l][:, X_WIDTH:].astype(BF16)
        conv_args = (w_dw[l].astype(F32), vec(b_dw[l]), vec(g_cln[l]), vec(b_cln[l]), vec(g_conv_out[l]))
        last = l == depth - 1

        mk_p, mv_p = _memory_kv(mem2, vec(g_mem[l]), wk_mem, wv_mem, _row_tile(nb * n_mem, 512))
        outs["mkp"].append(mk_p.reshape(nb, n_mem, X_HEADS, X_HEAD_DIM))
        outs["mvp"].append(mv_p.reshape(nb, n_mem, X_HEADS, X_HEAD_DIM))
        mems = [(mk_p.reshape(nb, n_mem, X_WIDTH), mv_p.reshape(nb, n_mem, X_WIDTH)),
                (cache_mem_k[l].reshape(nd, n_mem, X_WIDTH), cache_mem_v[l].reshape(nd, n_mem, X_WIDTH))]

        for grp in range(2):
            x = xs[grp]
            m = x.shape[0]
            tm = _row_tile(m, 512)
            q, k, v, kb, vb, lf_pad, u = _proj_in(x, vec(g_mix[l]), wqkv, wf, wglu, bf, tm)
            lf = lf_pad[:, :N_HEADS]
            if grp == 0:
                c = _cumsum_logf(lf_pad, nb, seq)
                o_attn = _fox_prompt(q, kb, vb, c, nb, seq, _row_tile(seq, 512))
                u3 = u.reshape(nb, seq, cw)
                o_conv = _conv_prompt(u3, *conv_args, _row_tile(seq, 512)).reshape(m, cw)
                outs["kp"].append(k.reshape(nb, seq, N_HEADS, HEAD_DIM))
                outs["vp"].append(v.reshape(nb, seq, N_HEADS, HEAD_DIM))
                outs["lfp"].append(lf.reshape(nb, seq, N_HEADS))
                outs["cvp"].append(u3[:, seq - hist:])
                n_grp, t_grp = nb, seq
            else:
                new_t = lambda z: jnp.pad(z.reshape(nd, t_new, -1).transpose(0, 2, 1),
                                          ((0, 0), (0, 0), (0, page - t_new)))
                o_attn = _fox_sample(page_table, l * n_phys, q, new_t(k).reshape(nd * a, page),
                                     new_t(v).reshape(nd * a, page), new_t(lf),
                                     cache_kt, cache_vt, cache_lft, t_new)
                u_hist = jnp.concatenate([cache_conv[l].astype(F32), u.reshape(nd, t_new, cw)], axis=1)
                o_conv = _conv_sample(u_hist, *conv_args, t_new).reshape(m, cw)
                outs["ks"].append(k.reshape(nd, t_new, N_HEADS, HEAD_DIM))
                outs["vs"].append(v.reshape(nd, t_new, N_HEADS, HEAD_DIM))
                outs["lfs"].append(lf.reshape(nd, t_new, N_HEADS))
                outs["cvs"].append(u_hist[:, -hist:])
                n_grp, t_grp = nd, t_new
            x = _out_proj(x, o_attn, o_conv, vec(g_attn_out[l]), wa, wc, tm)
            x = _cross(x, vec(g_cross[l]), wq, mems[grp][0], mems[grp][1], wo, n_grp, _row_tile(t_grp, 512))
            if l % 2 == 0:
                x = _ffn(x, vec(g_ffn[l]), w_ff_gu[l // 2].astype(BF16), w_ff_down[l // 2].astype(BF16), tm, 256)
                if last:
                    x = _final_norm(x, vec(g_final), tm)
            else:
                e = l // 2
                gates, top = _router(x, vec(g_ffn[l]), pad_lanes(w_router[e].astype(F32)),
                                     pad_lanes(vec(b_router[e])), tm)
                wgu_e, wd_e = w_e_gu[e].astype(BF16), w_e_down[e].astype(BF16)
                if m >= MOE_SPARSE_MIN_TOKENS:
                    token_slot, slot_token, slot_gate, tile_expert, n_valid = _moe_routing(top, MOE_TILE)
                    xg = _moe_gather(slot_token, n_valid, x, vec(g_ffn[l]), MOE_TILE)
                    ys = _moe_sparse(tile_expert, n_valid, xg, slot_gate, wgu_e, wd_e, MOE_TILE, 512)
                    x = _moe_combine(token_slot, x, vec(g_final), ys, last, _row_tile(m, 256))
                else:
                    x = _moe(x, vec(g_ffn[l]), gates, wgu_e, wd_e, vec(g_final), last, tm, 512)
            xs[grp] = x

    st = lambda key: jnp.stack(outs[key])
    return (xs[0].reshape(nb, seq, d), xs[1].reshape(nd, t_new, d),
            st("kp"), st("vp"), st("lfp"), st("cvp"), st("mkp"), st("mvp"),
            st("ks"), st("vs"), st("lfs"), st("cvs"))
```

```python
import functools

import jax
import jax.numpy as jnp
from jax import lax
from jax.experimental import pallas as pl
from jax.experimental.pallas import tpu as pltpu

EPS = 1e-6
N_HEADS = 8
HEAD_DIM = 64
ATTN_WIDTH = N_HEADS * HEAD_DIM
CONV_K = 31
CONV_HALO = 32
X_HEADS = 4
X_HEAD_DIM = 128
X_WIDTH = X_HEADS * X_HEAD_DIM
N_EXPERTS = 8
LANES = 128
SUBLANES = 8
NEG = -1e30
VMEM_LIMIT_BYTES = 56 * 1024 * 1024
PAGES_PER_STEP = 16
MOE_TILE = 1024

F32 = jnp.float32
BF16 = jnp.bfloat16
HIGHEST = lax.Precision.HIGHEST


def _params(*sem):
    return pltpu.CompilerParams(dimension_semantics=sem, vmem_limit_bytes=VMEM_LIMIT_BYTES)


def _rms(x, g):
    return x * lax.rsqrt(jnp.mean(x * x, axis=-1, keepdims=True) + EPS) * g


def _sigmoid(x):
    return 1.0 / (1.0 + jnp.exp(-x))


def _dot(a, b):
    return jnp.dot(a, b, preferred_element_type=F32)


def _dot_nt(a, b):
    return lax.dot_general(a, b, (((1,), (1,)), ((), ())), preferred_element_type=F32)


def _full(shape):
    nd = len(shape)
    return pl.BlockSpec(shape, lambda *_: (0,) * nd)


def _proj_in_kernel(x_ref, g_ref, wqkv_ref, wf_ref, wglu_ref, bf_ref,
                    q_ref, k_ref, v_ref, kb_ref, vb_ref, lf_ref, u_ref, *, transposed):
    a = ATTN_WIDTH
    h = _rms(x_ref[...], g_ref[...]).astype(BF16)
    qkv = _dot(h, wqkv_ref[...])
    q_ref[...] = (qkv[:, :a] * (HEAD_DIM ** -0.5)).astype(BF16)
    k = qkv[:, a:2 * a]
    v = qkv[:, 2 * a:]
    if transposed:
        k_ref[0] = k.T
        v_ref[0] = v.T
    else:
        k_ref[...] = k
        v_ref[...] = v
    kb_ref[...] = k.astype(BF16)
    vb_ref[...] = v.astype(BF16)
    z = _dot(h, wf_ref[...]) + bf_ref[...]
    lf_ref[...] = jnp.minimum(z, 0.0) - jnp.log1p(jnp.exp(-jnp.abs(z)))
    glu = _dot(h, wglu_ref[...])
    c = glu.shape[1] // 2
    u_ref[...] = glu[:, :c] * _sigmoid(glu[:, c:])


def _proj_in(x, g, wqkv, wf, wglu, bf, tm, seq_len=None):
    m, d = x.shape
    a = ATTN_WIDTH
    c = wglu.shape[1] // 2
    row = lambda w: pl.BlockSpec((tm, w), lambda i: (i, 0))
    if seq_len is None:
        kv_spec, kv_shape = row(a), jax.ShapeDtypeStruct((m, a), F32)
    else:
        per = seq_len // tm
        kv_spec = pl.BlockSpec((1, a, tm), lambda i: (i // per, 0, i % per))
        kv_shape = jax.ShapeDtypeStruct((m // seq_len, a, seq_len), F32)
    return pl.pallas_call(
        functools.partial(_proj_in_kernel, transposed=seq_len is not None),
        grid=(m // tm,),
        in_specs=[row(d), _full((1, d)), _full(wqkv.shape), _full(wf.shape), _full(wglu.shape),
                  _full((1, LANES))],
        out_specs=[row(a), kv_spec, kv_spec, row(a), row(a), row(LANES), row(c)],
        out_shape=[jax.ShapeDtypeStruct((m, a), BF16), kv_shape, kv_shape, jax.ShapeDtypeStruct((m, a), BF16),
                   jax.ShapeDtypeStruct((m, a), BF16), jax.ShapeDtypeStruct((m, LANES), F32),
                   jax.ShapeDtypeStruct((m, c), F32)],
        compiler_params=_params("parallel"),
        name="proj_in",
    )(x, g, wqkv, wf, wglu, bf)


def _cumsum_kernel(lf_ref, c_ref):
    s = lf_ref.shape[0]
    lft = lf_ref[...].T[:N_HEADS, :]
    r = lax.broadcasted_iota(jnp.int32, (LANES, LANES), 0)
    cidx = lax.broadcasted_iota(jnp.int32, (LANES, LANES), 1)
    tri = (r <= cidx).astype(F32)
    carry = jnp.zeros((N_HEADS, 1), F32)
    for b in range(s // LANES):
        cb = jnp.dot(lft[:, b * LANES:(b + 1) * LANES], tri, precision=HIGHEST,
                     preferred_element_type=F32) + carry
        c_ref[0, :, b * LANES:(b + 1) * LANES] = cb
        carry = cb[:, LANES - 1:LANES]


def _cumsum_logf(lf_pad, nb, s):
    return pl.pallas_call(
        _cumsum_kernel,
        grid=(nb,),
        in_specs=[pl.BlockSpec((s, LANES), lambda b: (b, 0))],
        out_specs=pl.BlockSpec((1, N_HEADS, s), lambda b: (b, 0, 0)),
        out_shape=jax.ShapeDtypeStruct((nb, N_HEADS, s), F32),
        compiler_params=_params("parallel"),
        name="cumsum_logf",
    )(lf_pad)


def _fox_prompt_kernel(q_ref, k_ref, v_ref, c_ref, o_ref, m_scr, l_scr, acc_scr, *, t):
    i = pl.program_id(2)
    lane = lax.broadcasted_iota(jnp.int32, (1, LANES), 1)
    q = q_ref[...]
    zero = jnp.zeros_like(q)
    qh = (jnp.where(lane < HEAD_DIM, q, zero), jnp.where(lane >= HEAD_DIM, q, zero))
    m_scr[...] = jnp.full(m_scr.shape, NEG, F32)
    l_scr[...] = jnp.zeros(l_scr.shape, F32)
    acc_scr[...] = jnp.zeros(acc_scr.shape, F32)
    row = lax.broadcasted_iota(jnp.int32, (t, t), 0)
    col = lax.broadcasted_iota(jnp.int32, (t, t), 1)

    def step(j, masked):
        start = pl.multiple_of(j * t, t)
        kb = k_ref[pl.ds(start, t), :]
        vb = v_ref[pl.ds(start, t), :]
        cs = c_ref[:, j]
        for hh in range(2):
            s = _dot_nt(qh[hh], kb) - cs[hh]
            if masked:
                s = jnp.where(col <= row, s, NEG)
            m_prev = m_scr[hh]
            m_new = jnp.maximum(m_prev, jnp.max(s, axis=-1, keepdims=True))
            alpha = jnp.exp(m_prev - m_new)
            p = jnp.exp(s - jnp.concatenate([m_new] * (t // LANES), axis=1))
            l_scr[hh] = alpha * l_scr[hh] + jnp.sum(p, axis=-1, keepdims=True)
            acc_scr[hh] = alpha * acc_scr[hh] + _dot(p.astype(BF16), vb)
            m_scr[hh] = m_new

    def body(j, carry):
        step(j, False)
        return carry

    lax.fori_loop(0, i, body, 0)
    step(i, True)
    o_ref[...] = jnp.where(lane < HEAD_DIM, acc_scr[0] / l_scr[0], acc_scr[1] / l_scr[1])


def _fox_prompt(q, kb, vb, c, nb, s, t):
    m = q.shape[0]
    nt = s // t
    c4 = c.reshape(nb * N_HEADS, nt, 1, t)
    pairs = N_HEADS // 2
    return pl.pallas_call(
        functools.partial(_fox_prompt_kernel, t=t),
        grid=(nb, pairs, nt),
        in_specs=[pl.BlockSpec((t, LANES), lambda b, p, i: (b * nt + i, p)),
                  pl.BlockSpec((s, LANES), lambda b, p, i: (b, p)),
                  pl.BlockSpec((s, LANES), lambda b, p, i: (b, p)),
                  pl.BlockSpec((2, nt, 1, t), lambda b, p, i: (b * pairs + p, 0, 0, 0))],
        out_specs=pl.BlockSpec((t, LANES), lambda b, p, i: (b * nt + i, p)),
        out_shape=jax.ShapeDtypeStruct((m, ATTN_WIDTH), F32),
        scratch_shapes=[pltpu.VMEM((2, t, LANES), F32), pltpu.VMEM((2, t, LANES), F32),
                        pltpu.VMEM((2, t, LANES), F32)],
        compiler_params=_params("parallel", "parallel", "arbitrary"),
        name="fox_prompt",
    )(q, kb, vb, c4)


def _fox_sample_kernel(pt_ref, q_ref, kn_ref, vn_ref, lfn_ref, *rest, t_new, n_chunks):
    pg = PAGES_PER_STEP
    k_refs, v_refs, lf_refs = rest[:pg], rest[pg:2 * pg], rest[2 * pg:3 * pg]
    o_ref, m_scr, l_scr, acc_scr, carry_scr = rest[3 * pg:]
    del pt_ref
    j = pl.program_id(1)
    rows = N_HEADS * t_new
    width = ATTN_WIDTH
    page = lf_refs[0].shape[2]

    @pl.when(j == 0)
    def _():
        m_scr[...] = jnp.full(m_scr.shape, NEG, F32)
        l_scr[...] = jnp.zeros(l_scr.shape, F32)
        acc_scr[...] = jnp.zeros(acc_scr.shape, F32)
        carry_scr[...] = jnp.zeros(carry_scr.shape, F32)

    q = q_ref[...]
    q_rows = jnp.concatenate([q] * N_HEADS, axis=0)
    rr = lax.broadcasted_iota(jnp.int32, (rows, width), 0)
    ll = lax.broadcasted_iota(jnp.int32, (rows, width), 1)
    head_mask = (rr // t_new) == (ll // HEAD_DIM)
    q_rows = jnp.where(head_mask, q_rows, jnp.zeros_like(q_rows))

    def per_query_rows(x):
        return jnp.concatenate([jnp.broadcast_to(x[h:h + 1], (t_new, x.shape[1])) for h in range(N_HEADS)],
                               axis=0)

    def attend(kts, vts, biases):
        s = jnp.concatenate([_dot(q_rows, kt) + b for kt, b in zip(kts, biases)], axis=1)
        m_prev = m_scr[...]
        m_new = jnp.maximum(m_prev, jnp.max(s, axis=-1, keepdims=True))
        alpha = jnp.exp(m_prev - m_new)
        p = jnp.exp(s - m_new)
        l_scr[...] = alpha * l_scr[...] + jnp.sum(p, axis=-1, keepdims=True)
        p = p.astype(BF16)
        pv = _dot_nt(p[:, :page], vts[0])
        for g in range(1, len(vts)):
            pv = pv + _dot_nt(p[:, g * page:(g + 1) * page], vts[g])
        acc_scr[...] = alpha * acc_scr[...] + pv
        m_scr[...] = m_new

    lf = jnp.concatenate([lf_refs[g][0] for g in range(pg)], axis=0)
    lane_all = lax.broadcasted_iota(jnp.int32, lf.shape, 1)
    suffix = lf
    shift = 1
    while shift < page:
        nxt = pltpu.roll(suffix, page - shift, axis=1)
        suffix = suffix + jnp.where(lane_all + shift < page, nxt, 0.0)
        shift *= 2
    within = suffix - lf
    totals = jnp.sum(lf, axis=-1, keepdims=True)
    later = carry_scr[...]
    biases = [None] * pg
    for g in reversed(range(pg)):
        biases[g] = per_query_rows(within[g * N_HEADS:(g + 1) * N_HEADS] + later)
        later = later + totals[g * N_HEADS:(g + 1) * N_HEADS]
    carry_scr[...] = later
    attend([k_refs[g][0].astype(BF16) for g in range(pg)], [v_refs[g][0].astype(BF16) for g in range(pg)], biases)
    lane = lax.broadcasted_iota(jnp.int32, (N_HEADS, page), 1)

    @pl.when(j == n_chunks - 1)
    def _():
        c_new = lfn_ref[0]
        shift = 1
        while shift < t_new:
            c_new = c_new + jnp.where(lane >= shift, pltpu.roll(c_new, shift, axis=1), 0.0)
            shift *= 2
        br = lax.broadcasted_iota(jnp.int32, (rows, page), 0)
        bc = lax.broadcasted_iota(jnp.int32, (rows, page), 1)
        bias = jnp.where(bc <= (br % t_new), -per_query_rows(c_new), NEG)
        attend([kn_ref[...].astype(BF16)], [vn_ref[...].astype(BF16)], [bias])
        o = jnp.where(head_mask, acc_scr[...] / l_scr[...], 0.0)
        out = o[0:t_new]
        for h in range(1, N_HEADS):
            out = out + o[h * t_new:(h + 1) * t_new]
        o_ref[...] = out


def _fox_sample(page_table, base, q, kt_new, vt_new, lft_new, cache_kt, cache_vt, cache_lft, t_new):
    n, n_pages = page_table.shape
    pg = PAGES_PER_STEP
    n_chunks = n_pages // pg
    assert n_pages % pg == 0
    width, page = cache_kt.shape[1], cache_kt.shape[2]
    rows = N_HEADS * t_new

    def page_spec(shape, g):
        return pl.BlockSpec(shape, lambda b, j, pt: (base + pt[b, (n_chunks - 1 - j) * pg + g], 0, 0))

    in_specs = ([pl.BlockSpec((t_new, width), lambda b, j, pt: (b, 0)),
                 pl.BlockSpec((width, page), lambda b, j, pt: (b, 0)),
                 pl.BlockSpec((width, page), lambda b, j, pt: (b, 0)),
                 pl.BlockSpec((1, N_HEADS, page), lambda b, j, pt: (b, 0, 0))]
                + [page_spec((1, width, page), g) for g in range(pg)]
                + [page_spec((1, width, page), g) for g in range(pg)]
                + [page_spec((1, N_HEADS, page), g) for g in range(pg)])
    return pl.pallas_call(
        functools.partial(_fox_sample_kernel, t_new=t_new, n_chunks=n_chunks),
        grid_spec=pltpu.PrefetchScalarGridSpec(
            num_scalar_prefetch=1,
            grid=(n, n_chunks),
            in_specs=in_specs,
            out_specs=pl.BlockSpec((t_new, width), lambda b, j, pt: (b, 0)),
            scratch_shapes=[pltpu.VMEM((rows, 1), F32), pltpu.VMEM((rows, 1), F32),
                            pltpu.VMEM((rows, width), F32), pltpu.VMEM((N_HEADS, 1), F32)]),
        out_shape=jax.ShapeDtypeStruct((n * t_new, width), F32),
        compiler_params=_params("parallel", "arbitrary"),
        name="fox_sample",
    )(page_table, q, kt_new, vt_new, lft_new, *([cache_kt] * pg), *([cache_vt] * pg), *([cache_lft] * pg))


def _conv_tail(z, b_ref, gl_ref, bl_ref, gco_ref):
    z = z + b_ref[...]
    zc = z - jnp.mean(z, axis=-1, keepdims=True)
    y = zc * lax.rsqrt(jnp.mean(zc * zc, axis=-1, keepdims=True) + EPS) * gl_ref[...] + bl_ref[...]
    y = y * _sigmoid(y)
    return _rms(y, gco_ref[...]).astype(BF16)


def _conv_prompt_kernel(u_ref, halo_ref, w_ref, b_ref, gl_ref, bl_ref, gco_ref, o_ref, xs_ref, sh_ref, *, ts, rc):
    i = pl.program_id(1)
    halo = halo_ref[0]
    xs_ref[0:CONV_HALO, :] = jnp.where(i > 0, halo, jnp.zeros_like(halo))
    xs_ref[CONV_HALO:CONV_HALO + ts, :] = u_ref[0]
    n_sh = sh_ref.shape[1]
    for r in range(1, SUBLANES):
        sh_ref[r - 1] = xs_ref[r:r + n_sh, :]
    off = CONV_HALO - (CONV_K - 1)
    width = u_ref.shape[2]
    for c in range(ts // rc):
        acc = jnp.zeros((rc // SUBLANES, SUBLANES, width), F32)
        for j in range(CONV_K):
            q, r = divmod(off + j, SUBLANES)
            lo = c * rc + q * SUBLANES
            rows = xs_ref[lo:lo + rc, :] if r == 0 else sh_ref[r - 1, lo:lo + rc, :]
            acc = acc + w_ref[j][None] * rows.reshape(rc // SUBLANES, SUBLANES, width)
        o_ref[0, c * rc:(c + 1) * rc, :] = _conv_tail(acc.reshape(rc, width), b_ref, gl_ref, bl_ref, gco_ref)


def _conv_prompt(u3, w, b, gl, bl, gco, ts):
    nb, s, c = u3.shape
    hb = ts // CONV_HALO
    vec = _full((1, c))
    w = jnp.broadcast_to(w[:, None, :], (w.shape[0], SUBLANES, c))
    return pl.pallas_call(
        functools.partial(_conv_prompt_kernel, ts=ts, rc=32),
        grid=(nb, s // ts),
        in_specs=[pl.BlockSpec((1, ts, c), lambda n, i: (n, i, 0)),
                  pl.BlockSpec((1, CONV_HALO, c), lambda n, i: (n, jnp.maximum(i * hb - 1, 0), 0)),
                  _full(w.shape), vec, vec, vec, vec],
        out_specs=pl.BlockSpec((1, ts, c), lambda n, i: (n, i, 0)),
        out_shape=jax.ShapeDtypeStruct((nb, s, c), BF16),
        scratch_shapes=[pltpu.VMEM((CONV_HALO + ts, c), F32),
                        pltpu.VMEM((SUBLANES - 1, CONV_HALO + ts - SUBLANES, c), F32)],
        compiler_params=_params("parallel", "parallel"),
        name="conv_prompt",
    )(u3, u3, w, b, gl, bl, gco)


def _conv_sample_kernel(uh_ref, w_ref, b_ref, gl_ref, bl_ref, gco_ref, o_ref, *, t_new):
    acc = jnp.zeros((t_new, uh_ref.shape[2]), F32)
    for j in range(CONV_K):
        acc = acc + w_ref[j:j + 1, :] * uh_ref[0, j:j + t_new, :]
    o_ref[0] = _conv_tail(acc, b_ref, gl_ref, bl_ref, gco_ref)


def _conv_sample(u_hist, w, b, gl, bl, gco, t_new):
    n, hl, c = u_hist.shape
    vec = _full((1, c))
    return pl.pallas_call(
        functools.partial(_conv_sample_kernel, t_new=t_new),
        grid=(n,),
        in_specs=[pl.BlockSpec((1, hl, c), lambda i: (i, 0, 0)), _full(w.shape), vec, vec, vec, vec],
        out_specs=pl.BlockSpec((1, t_new, c), lambda i: (i, 0, 0)),
        out_shape=jax.ShapeDtypeStruct((n, t_new, c), BF16),
        compiler_params=_params("parallel"),
        name="conv_sample",
    )(u_hist, w, b, gl, bl, gco)


def _out_proj_kernel(x_ref, oa_ref, oc_ref, ga_ref, wa_ref, wc_ref, o_ref):
    a = _rms(oa_ref[...], ga_ref[...]).astype(BF16)
    o_ref[...] = x_ref[...] + _dot(a, wa_ref[...]) + _dot(oc_ref[...], wc_ref[...])


def _out_proj(x, o_attn, o_conv, ga, wa, wc, tm):
    m, d = x.shape
    row = lambda w: pl.BlockSpec((tm, w), lambda i: (i, 0))
    return pl.pallas_call(
        _out_proj_kernel,
        grid=(m // tm,),
        in_specs=[row(d), row(o_attn.shape[1]), row(o_conv.shape[1]), _full(ga.shape),
                  _full(wa.shape), _full(wc.shape)],
        out_specs=row(d),
        out_shape=jax.ShapeDtypeStruct((m, d), F32),
        compiler_params=_params("parallel"),
        name="out_proj",
    )(x, o_attn, o_conv, ga, wa, wc)


def _memory_kv_kernel(mem_ref, g_ref, wk_ref, wv_ref, mk_ref, mv_ref):
    h = _rms(mem_ref[...], g_ref[...]).astype(BF16)
    mk_ref[...] = _dot(h, wk_ref[...])
    mv_ref[...] = _dot(h, wv_ref[...])


def _memory_kv(mem, g, wk, wv, tm):
    m, d = mem.shape
    row = lambda w: pl.BlockSpec((tm, w), lambda i: (i, 0))
    return pl.pallas_call(
        _memory_kv_kernel,
        grid=(m // tm,),
        in_specs=[row(d), _full(g.shape), _full(wk.shape), _full(wv.shape)],
        out_specs=[row(X_WIDTH), row(X_WIDTH)],
        out_shape=[jax.ShapeDtypeStruct((m, X_WIDTH), F32)] * 2,
        compiler_params=_params("parallel"),
        name="memory_kv",
    )(mem, g, wk, wv)


def _cross_kernel(x_ref, g_ref, wq_ref, mk_ref, mv_ref, wo_ref, o_ref):
    x = x_ref[...]
    h = _rms(x, g_ref[...]).astype(BF16)
    q = (_dot(h, wq_ref[...]) * (X_HEAD_DIM ** -0.5)).astype(BF16)
    mk = mk_ref[0].astype(BF16)
    mv = mv_ref[0].astype(BF16)
    outs = []
    for hh in range(X_HEADS):
        sl = slice(hh * X_HEAD_DIM, (hh + 1) * X_HEAD_DIM)
        s = _dot_nt(q[:, sl], mk[:, sl])
        p = jnp.exp(s - jnp.max(s, axis=-1, keepdims=True))
        o = _dot(p.astype(BF16), mv[:, sl]) / jnp.sum(p, axis=-1, keepdims=True)
        outs.append(o.astype(BF16))
    o_ref[...] = x + _dot(jnp.concatenate(outs, axis=1), wo_ref[...])


def _cross(x, g, wq, mk, mv, wo, nb, tm):
    m, d = x.shape
    per = (m // nb) // tm
    n_mem = mk.shape[1]
    return pl.pallas_call(
        _cross_kernel,
        grid=(nb, per),
        in_specs=[pl.BlockSpec((tm, d), lambda b, i: (b * per + i, 0)), _full(g.shape), _full(wq.shape),
                  pl.BlockSpec((1, n_mem, X_WIDTH), lambda b, i: (b, 0, 0)),
                  pl.BlockSpec((1, n_mem, X_WIDTH), lambda b, i: (b, 0, 0)), _full(wo.shape)],
        out_specs=pl.BlockSpec((tm, d), lambda b, i: (b * per + i, 0)),
        out_shape=jax.ShapeDtypeStruct((m, d), F32),
        compiler_params=_params("parallel", "parallel"),
        name="cross_attn",
    )(x, g, wq, mk, mv, wo)


def _ffn_kernel(x_ref, g_ref, wgu_ref, wd_ref, o_ref, acc_ref, *, tf):
    x = x_ref[...]
    h = _rms(x, g_ref[...]).astype(BF16)
    f = wd_ref.shape[0]
    acc_ref[...] = x
    for c in range(f // tf):
        gt = _dot(h, wgu_ref[:, c * tf:(c + 1) * tf])
        up = _dot(h, wgu_ref[:, f + c * tf:f + (c + 1) * tf])
        act = (gt * _sigmoid(gt) * up).astype(BF16)
        acc_ref[...] += _dot(act, wd_ref[c * tf:(c + 1) * tf, :])
    o_ref[...] = acc_ref[...]


def _ffn(x, g, wgu, wd, tm, tf):
    m, d = x.shape
    row = pl.BlockSpec((tm, d), lambda i: (i, 0))
    resident = lambda shape: pl.BlockSpec(shape, lambda i: (0, 0), pipeline_mode=pl.Buffered(1))
    return pl.pallas_call(
        functools.partial(_ffn_kernel, tf=tf),
        grid=(m // tm,),
        in_specs=[row, _full(g.shape), resident(wgu.shape), resident(wd.shape)],
        out_specs=row,
        out_shape=jax.ShapeDtypeStruct((m, d), F32),
        scratch_shapes=[pltpu.VMEM((tm, d), F32)],
        compiler_params=_params("parallel"),
        name="ffn_dense",
    )(x, g, wgu, wd)


def _router_kernel(x_ref, g_ref, wr_ref, br_ref, top_ref):
    h = _rms(x_ref[...], g_ref[...])
    logits = jnp.dot(h, wr_ref[...], precision=HIGHEST, preferred_element_type=F32) + br_ref[...]
    lane = lax.broadcasted_iota(jnp.int32, logits.shape, 1)
    logits = jnp.where(lane < N_EXPERTS, logits, NEG)
    p = jnp.exp(logits - jnp.max(logits, axis=-1, keepdims=True))
    p = p / jnp.sum(p, axis=-1, keepdims=True)
    v1 = jnp.max(p, axis=-1, keepdims=True)
    i1 = jnp.min(jnp.where(p == v1, lane, LANES), axis=-1, keepdims=True)
    rest = jnp.where(lane == i1, -1.0, p)
    v2 = jnp.max(rest, axis=-1, keepdims=True)
    i2 = jnp.min(jnp.where(rest == v2, lane, LANES), axis=-1, keepdims=True)
    tot = v1 + v2
    g1 = v1 / tot
    g2 = v2 / tot
    top_ref[...] = jnp.where(lane == 0, i1.astype(F32),
                             jnp.where(lane == 1, i2.astype(F32),
                                       jnp.where(lane == 2, g1, jnp.where(lane == 3, g2, 0.0))))


def _router(x, g, wr, br, tm):
    m, d = x.shape
    return pl.pallas_call(
        _router_kernel,
        grid=(m // tm,),
        in_specs=[pl.BlockSpec((tm, d), lambda i: (i, 0)), _full(g.shape), _full(wr.shape), _full(br.shape)],
        out_specs=pl.BlockSpec((tm, LANES), lambda i: (i, 0)),
        out_shape=jax.ShapeDtypeStruct((m, LANES), F32),
        compiler_params=_params("parallel"),
        name="moe_router",
    )(x, g, wr, br)


def _to_tiles(ref, x):
    for a in range(SUBLANES):
        ref[pl.ds(a, x.shape[0], stride=SUBLANES), :] = x[:, a * LANES:(a + 1) * LANES]


def _from_tiles(ref):
    rows = ref.shape[0] // SUBLANES
    return jnp.concatenate([ref[pl.ds(a, rows, stride=SUBLANES), :] for a in range(SUBLANES)], axis=1)


def _tile_at(ref, slot):
    return ref.at[pl.ds(pl.multiple_of(slot * SUBLANES, SUBLANES), SUBLANES)]


def _moe_dispatch_kernel(s1_ref, s2_ref, x_ref, g_ref, xs_in, xs_hbm, stage, sem, *, td):
    del xs_in
    i = pl.program_id(0)
    _to_tiles(stage, _rms(x_ref[...], g_ref[...]))

    def issue(r, carry):
        src = _tile_at(stage, r)
        pltpu.make_async_copy(src, _tile_at(xs_hbm, s1_ref[i * td + r]), sem).start(priority=0)
        pltpu.make_async_copy(src, _tile_at(xs_hbm, s2_ref[i * td + r]), sem).start(priority=1)
        return carry

    lax.fori_loop(0, td, issue, 0, unroll=8)
    for _ in range(2):
        pltpu.make_async_copy(stage, xs_hbm.at[pl.ds(0, td * SUBLANES)], sem).wait()


def _moe_dispatch(slot1, slot2, x, g, xs, td):
    m, d = x.shape
    return pl.pallas_call(
        functools.partial(_moe_dispatch_kernel, td=td),
        grid_spec=pltpu.PrefetchScalarGridSpec(
            num_scalar_prefetch=2,
            grid=(m // td,),
            in_specs=[pl.BlockSpec((td, d), lambda i, *_: (i, 0)), pl.BlockSpec((1, d), lambda i, *_: (0, 0)),
                      pl.BlockSpec(memory_space=pl.ANY)],
            out_specs=pl.BlockSpec(memory_space=pl.ANY),
            scratch_shapes=[pltpu.VMEM((td * SUBLANES, LANES), F32), pltpu.SemaphoreType.DMA(())]),
        out_shape=jax.ShapeDtypeStruct(xs.shape, F32),
        input_output_aliases={4: 0},
        compiler_params=_params("arbitrary"),
        name="moe_dispatch",
    )(slot1, slot2, x, g, xs)


def _moe_sparse_kernel(te_ref, nvalid_ref, x_ref, wg_ref, wu_ref, wd_ref, o_ref, h_scr, acc_scr):
    del te_ref
    i = pl.program_id(0)
    f = pl.program_id(1)
    last = pl.num_programs(1) - 1
    valid = i < nvalid_ref[0]

    @pl.when(f == 0)
    def _():
        acc_scr[...] = jnp.zeros(acc_scr.shape, F32)
        h_scr[...] = _from_tiles(x_ref).astype(BF16)

    @pl.when(valid)
    def _():
        h = h_scr[...]
        gt = _dot(h, wg_ref[0].astype(BF16))
        up = _dot(h, wu_ref[0].astype(BF16))
        act = (gt * _sigmoid(gt) * up).astype(BF16)
        acc_scr[...] += _dot(act, wd_ref[0].astype(BF16))

    @pl.when(f == last)
    def _():
        _to_tiles(o_ref, acc_scr[...])


def _moe_sparse(tile_expert, n_valid, xs, wgu, wd, tm, tf):
    n_slots = xs.shape[0] // SUBLANES
    d, f_all = wd.shape[2], wd.shape[1]
    nf = f_all // tf

    def chunk(i, f, nv):
        return jnp.where(i < nv[0], f, nf - 1)

    tiles = pl.BlockSpec((tm * SUBLANES, LANES), lambda i, f, te, nv: (i, 0))
    return pl.pallas_call(
        _moe_sparse_kernel,
        grid_spec=pltpu.PrefetchScalarGridSpec(
            num_scalar_prefetch=2,
            grid=(n_slots // tm, nf),
            in_specs=[tiles,
                      pl.BlockSpec((1, d, tf), lambda i, f, te, nv: (te[i], 0, chunk(i, f, nv))),
                      pl.BlockSpec((1, d, tf), lambda i, f, te, nv: (te[i], 0, nf + chunk(i, f, nv))),
                      pl.BlockSpec((1, tf, d), lambda i, f, te, nv: (te[i], chunk(i, f, nv), 0))],
            out_specs=tiles,
            scratch_shapes=[pltpu.VMEM((tm, d), BF16), pltpu.VMEM((tm, d), F32)]),
        out_shape=jax.ShapeDtypeStruct(xs.shape, F32),
        compiler_params=_params("arbitrary", "arbitrary"),
        name="moe_sparse",
    )(tile_expert, n_valid, xs, wgu, wgu, wd)


def _moe_combine_kernel(s1_ref, s2_ref, x_ref, top_ref, gfin_ref, ys_hbm, o_ref, buf_a, buf_b, sem, *, tc, final):
    i = pl.program_id(0)

    def issue(r, carry):
        pltpu.make_async_copy(_tile_at(ys_hbm, s1_ref[i * tc + r]), _tile_at(buf_a, r), sem).start(priority=0)
        pltpu.make_async_copy(_tile_at(ys_hbm, s2_ref[i * tc + r]), _tile_at(buf_b, r), sem).start(priority=1)
        return carry

    lax.fori_loop(0, tc, issue, 0, unroll=8)
    pltpu.make_async_copy(ys_hbm.at[pl.ds(0, tc * SUBLANES)], buf_a, sem).wait()
    pltpu.make_async_copy(ys_hbm.at[pl.ds(0, tc * SUBLANES)], buf_b, sem).wait()
    top = top_ref[...]
    y = x_ref[...] + top[:, 2:3] * _from_tiles(buf_a) + top[:, 3:4] * _from_tiles(buf_b)
    o_ref[...] = _rms(y, gfin_ref[...]) if final else y


def _moe_combine(slot1, slot2, x, top, gfin, ys, final, tc):
    m, d = x.shape
    rows = lambda w: pl.BlockSpec((tc, w), lambda i, *_: (i, 0))
    return pl.pallas_call(
        functools.partial(_moe_combine_kernel, tc=tc, final=final),
        grid_spec=pltpu.PrefetchScalarGridSpec(
            num_scalar_prefetch=2,
            grid=(m // tc,),
            in_specs=[rows(d), rows(LANES), pl.BlockSpec((1, d), lambda i, *_: (0, 0)),
                      pl.BlockSpec(memory_space=pl.ANY)],
            out_specs=rows(d),
            scratch_shapes=[pltpu.VMEM((tc * SUBLANES, LANES), F32), pltpu.VMEM((tc * SUBLANES, LANES), F32),
                            pltpu.SemaphoreType.DMA(())]),
        out_shape=jax.ShapeDtypeStruct((m, d), F32),
        compiler_params=_params("arbitrary"),
        name="moe_combine",
    )(slot1, slot2, x, top, gfin, ys)


def _moe_routing(tops, tm):
    sizes = [t.shape[0] for t in tops]
    experts = jnp.concatenate([t[:, k] for k in range(2) for t in tops]).astype(jnp.int32)
    onehot = (experts[:, None] == jnp.arange(N_EXPERTS, dtype=jnp.int32)[None, :]).astype(jnp.int32)
    rank = jnp.sum((jnp.cumsum(onehot, axis=0) - onehot) * onehot, axis=1)
    counts = jnp.sum(onehot, axis=0)
    padded = ((counts + tm - 1) // tm) * tm
    ends = jnp.cumsum(padded)
    slot = ((ends - padded)[experts] + rank).astype(jnp.int32)
    n_tiles = -(-2 * sum(sizes) // tm) + N_EXPERTS
    tile_start = jnp.arange(n_tiles, dtype=jnp.int32) * tm
    tile_expert = jnp.minimum(jnp.sum((tile_start[:, None] >= ends[None, :]).astype(jnp.int32), axis=1),
                              N_EXPERTS - 1).astype(jnp.int32)
    n_valid = (ends[-1] // tm).astype(jnp.int32).reshape(1)
    offsets = [0]
    for s in sizes + sizes:
        offsets.append(offsets[-1] + s)
    pieces = [slot[offsets[j]:offsets[j + 1]] for j in range(2 * len(sizes))]
    slots = [(pieces[g], pieces[len(sizes) + g]) for g in range(len(sizes))]
    return slots, tile_expert, n_valid, n_tiles * tm


def _final_norm_kernel(x_ref, g_ref, o_ref):
    o_ref[...] = _rms(x_ref[...], g_ref[...])


def _final_norm(x, g, tm):
    m, d = x.shape
    row = pl.BlockSpec((tm, d), lambda i: (i, 0))
    return pl.pallas_call(
        _final_norm_kernel, grid=(m // tm,), in_specs=[row, _full(g.shape)], out_specs=row,
        out_shape=jax.ShapeDtypeStruct((m, d), F32), compiler_params=_params("parallel"),
        name="final_norm",
    )(x, g)


def _row_tile(m, want):
    t = min(m, want)
    assert m % t == 0, (m, t)
    return t


def kernel(x_prompt, x_sample, cache_k, cache_v, cache_logf, cache_conv, cache_mem_k, cache_mem_v, page_table, mem_prompt, g_mix, w_in, b_fgate, w_dw, b_dw, g_cln, b_cln, g_attn_out, g_conv_out, w_out, g_cross, g_mem, w_cq, w_ckv, w_co, g_ffn, w_ff_gu, w_ff_down, w_router, b_router, w_e_gu, w_e_down, g_final):
    nb, seq, d = x_prompt.shape
    nd, t_new, _ = x_sample.shape
    depth = w_in.shape[0]
    a = ATTN_WIDTH
    cw = d - a
    n_mem = mem_prompt.shape[1]
    n_phys, page = cache_k.shape[1], cache_k.shape[2]
    hist = CONV_K - 1
    vec = lambda v: v.reshape(1, -1).astype(F32)
    pad_lanes = lambda v, fill=0.0: jnp.pad(v, [(0, 0)] * (v.ndim - 1) + [(0, LANES - v.shape[-1])],
                                            constant_values=fill)

    xs = [x_prompt.reshape(nb * seq, d), x_sample.reshape(nd * t_new, d)]
    mem2 = mem_prompt.reshape(nb * n_mem, d)
    cache_kt = cache_k.transpose(0, 1, 3, 4, 2).reshape(depth * n_phys, a, page)
    cache_vt = cache_v.transpose(0, 1, 3, 4, 2).reshape(depth * n_phys, a, page)
    cache_lft = cache_logf.astype(F32).transpose(0, 1, 3, 2).reshape(depth * n_phys, N_HEADS, page)
    outs = {k: [] for k in ("kp", "vp", "lfp", "cvp", "mkp", "mvp", "ks", "vs", "lfs", "cvs")}

    for l in range(depth):
        wqkv = w_in[l][:, :3 * a].astype(BF16)
        wf = pad_lanes(w_in[l][:, 3 * a:3 * a + N_HEADS]).astype(BF16)
        wglu = w_in[l][:, 3 * a + N_HEADS:].astype(BF16)
        bf = pad_lanes(vec(b_fgate[l]))
        wa = w_out[l][:a].astype(BF16)
        wc = w_out[l][a:].astype(BF16)
        wq = w_cq[l].astype(BF16)
        wo = w_co[l].astype(BF16)
        wk_mem = w_ckv[l][:, :X_WIDTH].astype(BF16)
        wv_mem = w_ckv[l][:, X_WIDTH:].astype(BF16)
        conv_args = (w_dw[l].astype(F32), vec(b_dw[l]), vec(g_cln[l]), vec(b_cln[l]), vec(g_conv_out[l]))
        last = l == depth - 1

        mk_p, mv_p = _memory_kv(mem2, vec(g_mem[l]), wk_mem, wv_mem, _row_tile(nb * n_mem, 512))
        outs["mkp"].append(mk_p.reshape(nb, n_mem, X_HEADS, X_HEAD_DIM))
        outs["mvp"].append(mv_p.reshape(nb, n_mem, X_HEADS, X_HEAD_DIM))
        mems = [(mk_p.reshape(nb, n_mem, X_WIDTH), mv_p.reshape(nb, n_mem, X_WIDTH)),
                (cache_mem_k[l].reshape(nd, n_mem, X_WIDTH), cache_mem_v[l].reshape(nd, n_mem, X_WIDTH))]

        tiles = [_row_tile(x.shape[0], 512) for x in xs]
        for grp in range(2):
            x = xs[grp]
            m = x.shape[0]
            tm = tiles[grp]
            q, k, v, kb, vb, lf_pad, u = _proj_in(x, vec(g_mix[l]), wqkv, wf, wglu, bf, tm,
                                                  seq if grp == 0 else None)
            lf = lf_pad[:, :N_HEADS]
            if grp == 0:
                c = _cumsum_logf(lf_pad, nb, seq)
                o_attn = _fox_prompt(q, kb, vb, c, nb, seq, _row_tile(seq, 512))
                u3 = u.reshape(nb, seq, cw)
                o_conv = _conv_prompt(u3, *conv_args, _row_tile(seq, 512)).reshape(m, cw)
                outs["kp"].append(k)
                outs["vp"].append(v)
                outs["lfp"].append(lf.reshape(nb, seq, N_HEADS))
                outs["cvp"].append(u3[:, seq - hist:])
                n_grp, t_grp = nb, seq
            else:
                new_t = lambda z: jnp.pad(z.reshape(nd, t_new, -1).transpose(0, 2, 1),
                                          ((0, 0), (0, 0), (0, page - t_new)))
                o_attn = _fox_sample(page_table, l * n_phys, q, new_t(k).reshape(nd * a, page),
                                     new_t(v).reshape(nd * a, page), new_t(lf),
                                     cache_kt, cache_vt, cache_lft, t_new)
                u_hist = jnp.concatenate([cache_conv[l].astype(F32), u.reshape(nd, t_new, cw)], axis=1)
                o_conv = _conv_sample(u_hist, *conv_args, t_new).reshape(m, cw)
                outs["ks"].append(k.reshape(nd, t_new, N_HEADS, HEAD_DIM))
                outs["vs"].append(v.reshape(nd, t_new, N_HEADS, HEAD_DIM))
                outs["lfs"].append(lf.reshape(nd, t_new, N_HEADS))
                outs["cvs"].append(u_hist[:, -hist:])
                n_grp, t_grp = nd, t_new
            x = _out_proj(x, o_attn, o_conv, vec(g_attn_out[l]), wa, wc, tm)
            xs[grp] = _cross(x, vec(g_cross[l]), wq, mems[grp][0], mems[grp][1], wo, n_grp,
                             _row_tile(t_grp, 512))

        g_f = vec(g_ffn[l])
        if l % 2 == 0:
            wgu_d, wd_d = w_ff_gu[l // 2].astype(BF16), w_ff_down[l // 2].astype(BF16)
            xs = [_ffn(x, g_f, wgu_d, wd_d, tm, 256) for x, tm in zip(xs, tiles)]
            if last:
                xs = [_final_norm(x, vec(g_final), tm) for x, tm in zip(xs, tiles)]
        else:
            e = l // 2
            w_r, b_r = pad_lanes(w_router[e].astype(F32)), pad_lanes(vec(b_router[e]))
            tops = [_router(x, g_f, w_r, b_r, tm) for x, tm in zip(xs, tiles)]
            slots, tile_expert, n_valid, n_slots = _moe_routing(tops, MOE_TILE)
            slot_rows = jnp.zeros((n_slots * SUBLANES, LANES), F32)
            for x, (s1, s2) in zip(xs, slots):
                slot_rows = _moe_dispatch(s1, s2, x, g_f, slot_rows, _row_tile(x.shape[0], 256))
            ys = _moe_sparse(tile_expert, n_valid, slot_rows, w_e_gu[e], w_e_down[e], MOE_TILE, 512)
            xs = [_moe_combine(s1, s2, x, top, vec(g_final), ys, last, _row_tile(x.shape[0], 256))
                  for x, top, (s1, s2) in zip(xs, tops, slots)]

    st = lambda key: jnp.stack(outs[key])
    kv_prompt = lambda key: st(key).reshape(depth, nb, N_HEADS, HEAD_DIM, seq).transpose(0, 1, 4, 2, 3)
    return (xs[0].reshape(nb, seq, d), xs[1].reshape(nd, t_new, d),
            kv_prompt("kp"), kv_prompt("vp"), st("lfp"), st("cvp"), st("mkp"), st("mvp"),
            st("ks"), st("vs"), st("lfs"), st("cvs"))
```

```python
import functools

import jax
import jax.numpy as jnp
from jax import lax
from jax.experimental import pallas as pl
from jax.experimental.pallas import tpu as pltpu

EPS = 1e-6
N_HEADS = 8
HEAD_DIM = 64
ATTN_WIDTH = N_HEADS * HEAD_DIM
CONV_K = 31
CONV_HALO = 32
X_HEADS = 4
X_HEAD_DIM = 128
X_WIDTH = X_HEADS * X_HEAD_DIM
N_EXPERTS = 8
LANES = 128
SUBLANES = 8
NEG = -1e30
LOG2E = 1.4426950408889634
VMEM_LIMIT_BYTES = 56 * 1024 * 1024
PAGES_PER_STEP = 16
MOE_TILE = 1024

F32 = jnp.float32
BF16 = jnp.bfloat16
HIGHEST = lax.Precision.HIGHEST


def _params(*sem):
    return pltpu.CompilerParams(dimension_semantics=sem, vmem_limit_bytes=VMEM_LIMIT_BYTES)


def _rms(x, g):
    return x * lax.rsqrt(jnp.mean(x * x, axis=-1, keepdims=True) + EPS) * g


def _sigmoid(x):
    return 1.0 / (1.0 + jnp.exp(-x))


def _dot(a, b):
    return jnp.dot(a, b, preferred_element_type=F32)


def _dot_nt(a, b):
    return lax.dot_general(a, b, (((1,), (1,)), ((), ())), preferred_element_type=F32)


def _full(shape):
    nd = len(shape)
    return pl.BlockSpec(shape, lambda *_: (0,) * nd)


def _proj_in_kernel(x_ref, g_ref, wqkv_ref, wf_ref, wglu_ref, bf_ref,
                    q_ref, k_ref, v_ref, kb_ref, vb_ref, lf_ref, u_ref, *, transposed):
    a = ATTN_WIDTH
    h = _rms(x_ref[...], g_ref[...]).astype(BF16)
    qkv = _dot(h, wqkv_ref[...])
    q_ref[...] = (qkv[:, :a] * (HEAD_DIM ** -0.5 * LOG2E)).astype(BF16)
    k = qkv[:, a:2 * a]
    v = qkv[:, 2 * a:]
    if transposed:
        k_ref[0, 0] = k.T
        v_ref[0, 0] = v.T
    else:
        k_ref[...] = k
        v_ref[...] = v
    kb_ref[...] = k.astype(BF16)
    vb_ref[...] = v.astype(BF16)
    z = _dot(h, wf_ref[...]) + bf_ref[...]
    lf_ref[...] = jnp.minimum(z, 0.0) - jnp.log1p(jnp.exp(-jnp.abs(z)))
    glu = _dot(h, wglu_ref[...])
    c = glu.shape[1] // 2
    u_ref[...] = glu[:, :c] * _sigmoid(glu[:, c:])


def _proj_in_carry_kernel(inner, x_ref, g_ref, wqkv_ref, wf_ref, wglu_ref, bf_ref, k_prev, v_prev,
                          q_ref, k_ref, v_ref, *out_refs):
    n_prev = k_prev.shape[0]
    k_ref[0:n_prev] = k_prev[...]
    v_ref[0:n_prev] = v_prev[...]
    inner(x_ref, g_ref, wqkv_ref, wf_ref, wglu_ref, bf_ref, q_ref, k_ref.at[n_prev:], v_ref.at[n_prev:], *out_refs)


def _proj_in(x, g, wqkv, wf, wglu, bf, tm, stacked=None):
    m, d = x.shape
    a = ATTN_WIDTH
    c = wglu.shape[1] // 2
    row = lambda w: pl.BlockSpec((tm, w), lambda i: (i, 0))
    args = [x, g, wqkv, wf, wglu, bf]
    in_specs = [row(d), _full((1, d)), _full(wqkv.shape), _full(wf.shape), _full(wglu.shape), _full((1, LANES))]
    if stacked is None:
        kv_spec, kv_shape = row(a), jax.ShapeDtypeStruct((m, a), F32)
        kernel_fn = functools.partial(_proj_in_kernel, transposed=False)
    else:
        seq_len, k_prev, v_prev = stacked
        per = seq_len // tm
        n_prev = 0 if k_prev is None else k_prev.shape[0]
        slabs = lambda n: pl.BlockSpec((n, 1, a, tm), lambda i: (0, i // per, 0, i % per))
        kv_spec = slabs(n_prev + 1)
        kv_shape = jax.ShapeDtypeStruct((n_prev + 1, m // seq_len, a, seq_len), F32)
        kernel_fn = functools.partial(_proj_in_kernel, transposed=True)
        if n_prev:
            args += [k_prev, v_prev]
            in_specs += [slabs(n_prev)] * 2
            kernel_fn = functools.partial(_proj_in_carry_kernel, kernel_fn)
    return pl.pallas_call(
        kernel_fn,
        grid=(m // tm,),
        in_specs=in_specs,
        out_specs=[row(a), kv_spec, kv_spec, row(a), row(a), row(LANES), row(c)],
        out_shape=[jax.ShapeDtypeStruct((m, a), BF16), kv_shape, kv_shape, jax.ShapeDtypeStruct((m, a), BF16),
                   jax.ShapeDtypeStruct((m, a), BF16), jax.ShapeDtypeStruct((m, LANES), F32),
                   jax.ShapeDtypeStruct((m, c), F32)],
        compiler_params=_params("parallel"),
        name="proj_in",
    )(*args)


def _cumsum_kernel(lf_ref, c_ref):
    s = lf_ref.shape[0]
    lft = lf_ref[...].T[:N_HEADS, :]
    r = lax.broadcasted_iota(jnp.int32, (LANES, LANES), 0)
    cidx = lax.broadcasted_iota(jnp.int32, (LANES, LANES), 1)
    tri = (r <= cidx).astype(F32)
    carry = jnp.zeros((N_HEADS, 1), F32)
    for b in range(s // LANES):
        cb = jnp.dot(lft[:, b * LANES:(b + 1) * LANES], tri, precision=HIGHEST,
                     preferred_element_type=F32) + carry
        c_ref[0, :, b * LANES:(b + 1) * LANES] = cb * LOG2E
        carry = cb[:, LANES - 1:LANES]


def _cumsum_logf(lf_pad, nb, s):
    return pl.pallas_call(
        _cumsum_kernel,
        grid=(nb,),
        in_specs=[pl.BlockSpec((s, LANES), lambda b: (b, 0))],
        out_specs=pl.BlockSpec((1, N_HEADS, s), lambda b: (b, 0, 0)),
        out_shape=jax.ShapeDtypeStruct((nb, N_HEADS, s), F32),
        compiler_params=_params("parallel"),
        name="cumsum_logf",
    )(lf_pad)


def _fox_prompt_kernel(q_ref, k_ref, v_ref, c_ref, o_ref, m_scr, l_scr, acc_scr, *, tq, tk):
    i = pl.program_id(2)
    lane = lax.broadcasted_iota(jnp.int32, (1, LANES), 1)
    q = q_ref[...]
    zero = jnp.zeros_like(q)
    qh = (jnp.where(lane < HEAD_DIM, q, zero), jnp.where(lane >= HEAD_DIM, q, zero))
    m_scr[...] = jnp.full(m_scr.shape, NEG, F32)
    l_scr[...] = jnp.zeros(l_scr.shape, F32)
    acc_scr[...] = jnp.zeros(acc_scr.shape, F32)
    row = lax.broadcasted_iota(jnp.int32, (tq, tk), 0)
    col = lax.broadcasted_iota(jnp.int32, (tq, tk), 1)
    ratio = tq // tk

    def step(j, diag):
        start = pl.multiple_of(j * tk, tk)
        kb = k_ref[pl.ds(start, tk), :]
        vb = v_ref[pl.ds(start, tk), :]
        cs = c_ref[:, j]
        for hh in range(2):
            s = _dot_nt(qh[hh], kb) - cs[hh]
            if diag is not None:
                s = jnp.where(col + diag * tk <= row, s, NEG)
            m_prev = m_scr[hh]
            m_new = jnp.maximum(m_prev, jnp.max(s, axis=-1, keepdims=True))
            alpha = jnp.exp2(m_prev - m_new)
            p = jnp.exp2(s - jnp.concatenate([m_new] * (tk // LANES), axis=1))
            l_scr[hh] = alpha * l_scr[hh] + jnp.sum(p, axis=-1, keepdims=True)
            acc_scr[hh] = alpha * acc_scr[hh] + _dot(p.astype(BF16), vb)
            m_scr[hh] = m_new

    def body(j, carry):
        step(j, None)
        return carry

    lax.fori_loop(0, i * ratio, body, 0)
    for dg in range(ratio):
        step(i * ratio + dg, dg)
    o_ref[...] = jnp.where(lane < HEAD_DIM, acc_scr[0] / l_scr[0], acc_scr[1] / l_scr[1])


def _fox_prompt(q, kb, vb, c, nb, s, tq, tk):
    m = q.shape[0]
    nt = s // tq
    nk = s // tk
    c4 = c.reshape(nb * N_HEADS, nk, 1, tk)
    pairs = N_HEADS // 2
    return pl.pallas_call(
        functools.partial(_fox_prompt_kernel, tq=tq, tk=tk),
        grid=(nb, pairs, nt),
        in_specs=[pl.BlockSpec((tq, LANES), lambda b, p, i: (b * nt + i, p)),
                  pl.BlockSpec((s, LANES), lambda b, p, i: (b, p)),
                  pl.BlockSpec((s, LANES), lambda b, p, i: (b, p)),
                  pl.BlockSpec((2, nk, 1, tk), lambda b, p, i: (b * pairs + p, 0, 0, 0))],
        out_specs=pl.BlockSpec((tq, LANES), lambda b, p, i: (b * nt + i, p)),
        out_shape=jax.ShapeDtypeStruct((m, ATTN_WIDTH), F32),
        scratch_shapes=[pltpu.VMEM((2, tq, LANES), F32), pltpu.VMEM((2, tq, LANES), F32),
                        pltpu.VMEM((2, tq, LANES), F32)],
        compiler_params=_params("parallel", "parallel", "arbitrary"),
        name="fox_prompt",
    )(q, kb, vb, c4)


def _fox_sample_kernel(pt_ref, q_ref, kn_ref, vn_ref, lfn_ref, *rest, t_new, n_chunks):
    pg = PAGES_PER_STEP
    k_refs, v_refs, lf_refs = rest[:pg], rest[pg:2 * pg], rest[2 * pg:3 * pg]
    o_ref, m_scr, l_scr, acc_scr, carry_scr = rest[3 * pg:]
    del pt_ref
    j = pl.program_id(1)
    rows = N_HEADS * t_new
    width = ATTN_WIDTH
    page = lf_refs[0].shape[2]

    @pl.when(j == 0)
    def _():
        m_scr[...] = jnp.full(m_scr.shape, NEG, F32)
        l_scr[...] = jnp.zeros(l_scr.shape, F32)
        acc_scr[...] = jnp.zeros(acc_scr.shape, F32)
        carry_scr[...] = jnp.zeros(carry_scr.shape, F32)

    q = q_ref[...]
    q_rows = jnp.concatenate([q] * N_HEADS, axis=0)
    rr = lax.broadcasted_iota(jnp.int32, (rows, width), 0)
    ll = lax.broadcasted_iota(jnp.int32, (rows, width), 1)
    head_mask = (rr // t_new) == (ll // HEAD_DIM)
    q_rows = jnp.where(head_mask, q_rows, jnp.zeros_like(q_rows))

    def per_query_rows(x):
        return jnp.concatenate([jnp.broadcast_to(x[h:h + 1], (t_new, x.shape[1])) for h in range(N_HEADS)],
                               axis=0)

    def attend(kts, vts, biases):
        s = jnp.concatenate([_dot(q_rows, kt) + b for kt, b in zip(kts, biases)], axis=1)
        m_prev = m_scr[...]
        m_new = jnp.maximum(m_prev, jnp.max(s, axis=-1, keepdims=True))
        alpha = jnp.exp2(m_prev - m_new)
        p = jnp.exp2(s - m_new)
        l_scr[...] = alpha * l_scr[...] + jnp.sum(p, axis=-1, keepdims=True)
        p = p.astype(BF16)
        pv = _dot_nt(p[:, :page], vts[0])
        for g in range(1, len(vts)):
            pv = pv + _dot_nt(p[:, g * page:(g + 1) * page], vts[g])
        acc_scr[...] = alpha * acc_scr[...] + pv
        m_scr[...] = m_new

    lf = jnp.concatenate([lf_refs[g][0] for g in range(pg)], axis=0)
    lane_all = lax.broadcasted_iota(jnp.int32, lf.shape, 1)
    suffix = lf
    shift = 1
    while shift < page:
        nxt = pltpu.roll(suffix, page - shift, axis=1)
        suffix = suffix + jnp.where(lane_all + shift < page, nxt, 0.0)
        shift *= 2
    within = suffix - lf
    totals = jnp.sum(lf, axis=-1, keepdims=True)
    later = carry_scr[...]
    biases = [None] * pg
    for g in reversed(range(pg)):
        biases[g] = per_query_rows((within[g * N_HEADS:(g + 1) * N_HEADS] + later) * LOG2E)
        later = later + totals[g * N_HEADS:(g + 1) * N_HEADS]
    carry_scr[...] = later
    attend([k_refs[g][0].astype(BF16) for g in range(pg)], [v_refs[g][0].astype(BF16) for g in range(pg)], biases)
    lane = lax.broadcasted_iota(jnp.int32, (N_HEADS, page), 1)

    @pl.when(j == n_chunks - 1)
    def _():
        c_new = lfn_ref[0]
        shift = 1
        while shift < t_new:
            c_new = c_new + jnp.where(lane >= shift, pltpu.roll(c_new, shift, axis=1), 0.0)
            shift *= 2
        br = lax.broadcasted_iota(jnp.int32, (rows, page), 0)
        bc = lax.broadcasted_iota(jnp.int32, (rows, page), 1)
        bias = jnp.where(bc <= (br % t_new), per_query_rows(c_new * (-LOG2E)), NEG)
        attend([kn_ref[...].astype(BF16)], [vn_ref[...].astype(BF16)], [bias])
        o = jnp.where(head_mask, acc_scr[...] / l_scr[...], 0.0)
        out = o[0:t_new]
        for h in range(1, N_HEADS):
            out = out + o[h * t_new:(h + 1) * t_new]
        o_ref[...] = out


def _fox_sample(page_table, base, q, kt_new, vt_new, lft_new, cache_kt, cache_vt, cache_lft, t_new):
    n, n_pages = page_table.shape
    pg = PAGES_PER_STEP
    n_chunks = n_pages // pg
    assert n_pages % pg == 0
    width, page = cache_kt.shape[1], cache_kt.shape[2]
    rows = N_HEADS * t_new

    def page_spec(shape, g):
        return pl.BlockSpec(shape, lambda b, j, pt: (base + pt[b, (n_chunks - 1 - j) * pg + g], 0, 0))

    in_specs = ([pl.BlockSpec((t_new, width), lambda b, j, pt: (b, 0)),
                 pl.BlockSpec((width, page), lambda b, j, pt: (b, 0)),
                 pl.BlockSpec((width, page), lambda b, j, pt: (b, 0)),
                 pl.BlockSpec((1, N_HEADS, page), lambda b, j, pt: (b, 0, 0))]
                + [page_spec((1, width, page), g) for g in range(pg)]
                + [page_spec((1, width, page), g) for g in range(pg)]
                + [page_spec((1, N_HEADS, page), g) for g in range(pg)])
    return pl.pallas_call(
        functools.partial(_fox_sample_kernel, t_new=t_new, n_chunks=n_chunks),
        grid_spec=pltpu.PrefetchScalarGridSpec(
            num_scalar_prefetch=1,
            grid=(n, n_chunks),
            in_specs=in_specs,
            out_specs=pl.BlockSpec((t_new, width), lambda b, j, pt: (b, 0)),
            scratch_shapes=[pltpu.VMEM((rows, 1), F32), pltpu.VMEM((rows, 1), F32),
                            pltpu.VMEM((rows, width), F32), pltpu.VMEM((N_HEADS, 1), F32)]),
        out_shape=jax.ShapeDtypeStruct((n * t_new, width), F32),
        compiler_params=_params("parallel", "arbitrary"),
        name="fox_sample",
    )(page_table, q, kt_new, vt_new, lft_new, *([cache_kt] * pg), *([cache_vt] * pg), *([cache_lft] * pg))


def _conv_tail(z, b_ref, gl_ref, bl_ref, gco_ref):
    z = z + b_ref[...]
    zc = z - jnp.mean(z, axis=-1, keepdims=True)
    y = zc * lax.rsqrt(jnp.mean(zc * zc, axis=-1, keepdims=True) + EPS) * gl_ref[...] + bl_ref[...]
    y = y * _sigmoid(y)
    return _rms(y, gco_ref[...]).astype(BF16)


def _conv_prompt_kernel(u_ref, halo_ref, w_ref, b_ref, gl_ref, bl_ref, gco_ref, o_ref, xs_ref, sh_ref, *, ts, rc):
    i = pl.program_id(1)
    halo = halo_ref[0]
    xs_ref[0:CONV_HALO, :] = jnp.where(i > 0, halo, jnp.zeros_like(halo))
    xs_ref[CONV_HALO:CONV_HALO + ts, :] = u_ref[0]
    n_sh = sh_ref.shape[1]
    for r in range(1, SUBLANES):
        sh_ref[r - 1] = xs_ref[r:r + n_sh, :]
    off = CONV_HALO - (CONV_K - 1)
    width = u_ref.shape[2]
    for c in range(ts // rc):
        acc = jnp.zeros((rc // SUBLANES, SUBLANES, width), F32)
        for j in range(CONV_K):
            q, r = divmod(off + j, SUBLANES)
            lo = c * rc + q * SUBLANES
            rows = xs_ref[lo:lo + rc, :] if r == 0 else sh_ref[r - 1, lo:lo + rc, :]
            acc = acc + w_ref[j][None] * rows.reshape(rc // SUBLANES, SUBLANES, width)
        o_ref[0, c * rc:(c + 1) * rc, :] = _conv_tail(acc.reshape(rc, width), b_ref, gl_ref, bl_ref, gco_ref)


def _conv_prompt(u3, w, b, gl, bl, gco, ts):
    nb, s, c = u3.shape
    hb = ts // CONV_HALO
    vec = _full((1, c))
    w = jnp.broadcast_to(w[:, None, :], (w.shape[0], SUBLANES, c))
    return pl.pallas_call(
        functools.partial(_conv_prompt_kernel, ts=ts, rc=32),
        grid=(nb, s // ts),
        in_specs=[pl.BlockSpec((1, ts, c), lambda n, i: (n, i, 0)),
                  pl.BlockSpec((1, CONV_HALO, c), lambda n, i: (n, jnp.maximum(i * hb - 1, 0), 0)),
                  _full(w.shape), vec, vec, vec, vec],
        out_specs=pl.BlockSpec((1, ts, c), lambda n, i: (n, i, 0)),
        out_shape=jax.ShapeDtypeStruct((nb, s, c), BF16),
        scratch_shapes=[pltpu.VMEM((CONV_HALO + ts, c), F32),
                        pltpu.VMEM((SUBLANES - 1, CONV_HALO + ts - SUBLANES, c), F32)],
        compiler_params=_params("parallel", "parallel"),
        name="conv_prompt",
    )(u3, u3, w, b, gl, bl, gco)


def _conv_sample_kernel(uh_ref, w_ref, b_ref, gl_ref, bl_ref, gco_ref, o_ref, *, t_new):
    acc = jnp.zeros((t_new, uh_ref.shape[2]), F32)
    for j in range(CONV_K):
        acc = acc + w_ref[j:j + 1, :] * uh_ref[0, j:j + t_new, :]
    o_ref[0] = _conv_tail(acc, b_ref, gl_ref, bl_ref, gco_ref)


def _conv_sample(u_hist, w, b, gl, bl, gco, t_new):
    n, hl, c = u_hist.shape
    vec = _full((1, c))
    return pl.pallas_call(
        functools.partial(_conv_sample_kernel, t_new=t_new),
        grid=(n,),
        in_specs=[pl.BlockSpec((1, hl, c), lambda i: (i, 0, 0)), _full(w.shape), vec, vec, vec, vec],
        out_specs=pl.BlockSpec((1, t_new, c), lambda i: (i, 0, 0)),
        out_shape=jax.ShapeDtypeStruct((n, t_new, c), BF16),
        compiler_params=_params("parallel"),
        name="conv_sample",
    )(u_hist, w, b, gl, bl, gco)


def _memory_kv_kernel(mem_ref, g_ref, wk_ref, wv_ref, mk_ref, mv_ref):
    h = _rms(mem_ref[...], g_ref[...]).astype(BF16)
    rows = mem_ref.shape[0]
    for ref, w_ref in ((mk_ref, wk_ref), (mv_ref, wv_ref)):
        kv = _dot(h, w_ref[...])
        for hh in range(X_HEADS):
            ref[pl.ds(hh, rows, stride=X_HEADS), :] = kv[:, hh * X_HEAD_DIM:(hh + 1) * X_HEAD_DIM]


def _memory_kv(mem, g, wk, wv, tm):
    m, d = mem.shape
    out = pl.BlockSpec((tm * X_HEADS, X_HEAD_DIM), lambda i: (i, 0))
    return pl.pallas_call(
        _memory_kv_kernel,
        grid=(m // tm,),
        in_specs=[pl.BlockSpec((tm, d), lambda i: (i, 0)), _full(g.shape), _full(wk.shape), _full(wv.shape)],
        out_specs=[out, out],
        out_shape=[jax.ShapeDtypeStruct((m * X_HEADS, X_HEAD_DIM), F32)] * 2,
        compiler_params=_params("parallel"),
        name="memory_kv",
    )(mem, g, wk, wv)


def _mix_cross_kernel(x_ref, oa_ref, oc_ref, ga_ref, wa_ref, wc_ref, g_ref, wq_ref, mk_ref, mv_ref, wo_ref,
                      *rest, routed):
    o_ref = rest[3] if routed else rest[0]
    a = _rms(oa_ref[...], ga_ref[...]).astype(BF16)
    x = x_ref[...] + _dot(a, wa_ref[...]) + _dot(oc_ref[...], wc_ref[...])
    h = _rms(x, g_ref[...]).astype(BF16)
    q = (_dot(h, wq_ref[...]) * (X_HEAD_DIM ** -0.5)).astype(BF16)
    n_mem = mk_ref.shape[1] // X_HEADS
    outs = []
    for hh in range(X_HEADS):
        head = pl.ds(hh, n_mem, stride=X_HEADS)
        s = _dot_nt(q[:, hh * X_HEAD_DIM:(hh + 1) * X_HEAD_DIM], mk_ref[0, head, :].astype(BF16))
        p = jnp.exp(s - jnp.max(s, axis=-1, keepdims=True))
        o = _dot(p.astype(BF16), mv_ref[0, head, :].astype(BF16)) / jnp.sum(p, axis=-1, keepdims=True)
        outs.append(o.astype(BF16))
    x = x + _dot(jnp.concatenate(outs, axis=1), wo_ref[...])
    o_ref[...] = x
    if routed:
        gf_ref, wr_ref, br_ref, _, top_ref = rest
        top_ref[...] = _route(x, gf_ref[...], wr_ref[...], br_ref[...])


def _mix_cross(x, o_attn, o_conv, ga, wa, wc, g, wq, mk, mv, wo, nb, base, tm, router=None):
    m, d = x.shape
    per = (m // nb) // tm
    mem_rows = mk.shape[1]
    rows = lambda w: pl.BlockSpec((tm, w), lambda b, i: (b * per + i, 0))
    mem = pl.BlockSpec((1, mem_rows, X_HEAD_DIM), lambda b, i: (base + b, 0, 0))
    args = [x, o_attn, o_conv, ga, wa, wc, g, wq, mk, mv, wo]
    in_specs = [rows(d), rows(o_attn.shape[1]), rows(o_conv.shape[1]), _full(ga.shape), _full(wa.shape),
                _full(wc.shape), _full(g.shape), _full(wq.shape), mem, mem, _full(wo.shape)]
    out_specs, out_shape = [rows(d)], [jax.ShapeDtypeStruct((m, d), F32)]
    if router is not None:
        args += list(router)
        in_specs += [_full(r.shape) for r in router]
        out_specs.append(rows(LANES))
        out_shape.append(jax.ShapeDtypeStruct((m, LANES), F32))
    res = pl.pallas_call(
        functools.partial(_mix_cross_kernel, routed=router is not None),
        grid=(nb, per),
        in_specs=in_specs,
        out_specs=out_specs,
        out_shape=out_shape,
        compiler_params=_params("parallel", "parallel"),
        name="mix_cross",
    )(*args)
    return (res[0], res[1]) if router is not None else (res[0], None)


def _ffn_kernel(x_ref, g_ref, wgu_ref, wd_ref, o_ref, acc_ref, *, tf):
    x = x_ref[...]
    h = _rms(x, g_ref[...]).astype(BF16)
    f = wd_ref.shape[0]
    acc_ref[...] = x
    for c in range(f // tf):
        gt = _dot(h, wgu_ref[:, c * tf:(c + 1) * tf])
        up = _dot(h, wgu_ref[:, f + c * tf:f + (c + 1) * tf])
        act = (gt * _sigmoid(gt) * up).astype(BF16)
        acc_ref[...] += _dot(act, wd_ref[c * tf:(c + 1) * tf, :])
    o_ref[...] = acc_ref[...]


def _ffn(x, g, wgu, wd, tm, tf):
    m, d = x.shape
    row = pl.BlockSpec((tm, d), lambda i: (i, 0))
    resident = lambda shape: pl.BlockSpec(shape, lambda i: (0, 0), pipeline_mode=pl.Buffered(1))
    return pl.pallas_call(
        functools.partial(_ffn_kernel, tf=tf),
        grid=(m // tm,),
        in_specs=[row, _full(g.shape), resident(wgu.shape), resident(wd.shape)],
        out_specs=row,
        out_shape=jax.ShapeDtypeStruct((m, d), F32),
        scratch_shapes=[pltpu.VMEM((tm, d), F32)],
        compiler_params=_params("parallel"),
        name="ffn_dense",
    )(x, g, wgu, wd)


def _route(x, g, wr, br):
    h = _rms(x, g)
    logits = jnp.dot(h, wr, precision=HIGHEST, preferred_element_type=F32) + br
    lane = lax.broadcasted_iota(jnp.int32, logits.shape, 1)
    logits = jnp.where(lane < N_EXPERTS, logits, NEG)
    p = jnp.exp(logits - jnp.max(logits, axis=-1, keepdims=True))
    p = p / jnp.sum(p, axis=-1, keepdims=True)
    v1 = jnp.max(p, axis=-1, keepdims=True)
    i1 = jnp.min(jnp.where(p == v1, lane, LANES), axis=-1, keepdims=True)
    rest = jnp.where(lane == i1, -1.0, p)
    v2 = jnp.max(rest, axis=-1, keepdims=True)
    i2 = jnp.min(jnp.where(rest == v2, lane, LANES), axis=-1, keepdims=True)
    tot = v1 + v2
    g1 = v1 / tot
    g2 = v2 / tot
    return jnp.where(lane == 0, i1.astype(F32),
                     jnp.where(lane == 1, i2.astype(F32), jnp.where(lane == 2, g1, jnp.where(lane == 3, g2, 0.0))))


def _to_tiles(ref, x):
    for a in range(SUBLANES):
        ref[pl.ds(a, x.shape[0], stride=SUBLANES), :] = x[:, a * LANES:(a + 1) * LANES]


def _from_tiles(ref):
    rows = ref.shape[0] // SUBLANES
    return jnp.concatenate([ref[pl.ds(a, rows, stride=SUBLANES), :] for a in range(SUBLANES)], axis=1)


def _tile_at(ref, slot):
    return ref.at[pl.ds(pl.multiple_of(slot * SUBLANES, SUBLANES), SUBLANES)]


def _moe_dispatch_kernel(s1_ref, s2_ref, x_ref, g_ref, xs_in, xs_hbm, stage, sem, *, td):
    del xs_in
    i = pl.program_id(0)
    _to_tiles(stage, _rms(x_ref[...], g_ref[...]))

    def issue(r, carry):
        src = _tile_at(stage, r)
        pltpu.make_async_copy(src, _tile_at(xs_hbm, s1_ref[i * td + r]), sem).start(priority=0)
        pltpu.make_async_copy(src, _tile_at(xs_hbm, s2_ref[i * td + r]), sem).start(priority=1)
        return carry

    lax.fori_loop(0, td, issue, 0, unroll=8)
    for _ in range(2):
        pltpu.make_async_copy(stage, xs_hbm.at[pl.ds(0, td * SUBLANES)], sem).wait()


def _moe_dispatch(slot1, slot2, x, g, xs, td):
    m, d = x.shape
    return pl.pallas_call(
        functools.partial(_moe_dispatch_kernel, td=td),
        grid_spec=pltpu.PrefetchScalarGridSpec(
            num_scalar_prefetch=2,
            grid=(m // td,),
            in_specs=[pl.BlockSpec((td, d), lambda i, *_: (i, 0)), pl.BlockSpec((1, d), lambda i, *_: (0, 0)),
                      pl.BlockSpec(memory_space=pl.ANY)],
            out_specs=pl.BlockSpec(memory_space=pl.ANY),
            scratch_shapes=[pltpu.VMEM((td * SUBLANES, LANES), F32), pltpu.SemaphoreType.DMA(())]),
        out_shape=jax.ShapeDtypeStruct(xs.shape, F32),
        input_output_aliases={4: 0},
        compiler_params=_params("arbitrary"),
        name="moe_dispatch",
    )(slot1, slot2, x, g, xs)


def _moe_sparse_kernel(te_ref, nvalid_ref, x_ref, wg_ref, wu_ref, wd_ref, o_ref, h_scr, acc_scr):
    del te_ref
    i = pl.program_id(0)
    f = pl.program_id(1)
    last = pl.num_programs(1) - 1
    valid = i < nvalid_ref[0]

    @pl.when(f == 0)
    def _():
        acc_scr[...] = jnp.zeros(acc_scr.shape, F32)
        h_scr[...] = _from_tiles(x_ref).astype(BF16)

    @pl.when(valid)
    def _():
        h = h_scr[...]
        gt = _dot(h, wg_ref[0].astype(BF16))
        up = _dot(h, wu_ref[0].astype(BF16))
        act = (gt * _sigmoid(gt) * up).astype(BF16)
        acc_scr[...] += _dot(act, wd_ref[0].astype(BF16))

    @pl.when(f == last)
    def _():
        _to_tiles(o_ref, acc_scr[...])


def _moe_sparse(tile_expert, n_valid, xs, wgu, wd, tm, tf):
    n_slots = xs.shape[0] // SUBLANES
    d, f_all = wd.shape[2], wd.shape[1]
    nf = f_all // tf

    def chunk(i, f, nv):
        return jnp.where(i < nv[0], f, nf - 1)

    tiles = pl.BlockSpec((tm * SUBLANES, LANES), lambda i, f, te, nv: (i, 0))
    return pl.pallas_call(
        _moe_sparse_kernel,
        grid_spec=pltpu.PrefetchScalarGridSpec(
            num_scalar_prefetch=2,
            grid=(n_slots // tm, nf),
            in_specs=[tiles,
                      pl.BlockSpec((1, d, tf), lambda i, f, te, nv: (te[i], 0, chunk(i, f, nv))),
                      pl.BlockSpec((1, d, tf), lambda i, f, te, nv: (te[i], 0, nf + chunk(i, f, nv))),
                      pl.BlockSpec((1, tf, d), lambda i, f, te, nv: (te[i], chunk(i, f, nv), 0))],
            out_specs=tiles,
            scratch_shapes=[pltpu.VMEM((tm, d), BF16), pltpu.VMEM((tm, d), F32)]),
        out_shape=jax.ShapeDtypeStruct(xs.shape, F32),
        compiler_params=_params("arbitrary", "arbitrary"),
        name="moe_sparse",
    )(tile_expert, n_valid, xs, wgu, wgu, wd)


def _moe_combine_kernel(s1_ref, s2_ref, x_ref, top_ref, gfin_ref, ys_hbm, o_ref, buf_a, buf_b, sem, *, tc, final):
    i = pl.program_id(0)

    def issue(r, carry):
        pltpu.make_async_copy(_tile_at(ys_hbm, s1_ref[i * tc + r]), _tile_at(buf_a, r), sem).start(priority=0)
        pltpu.make_async_copy(_tile_at(ys_hbm, s2_ref[i * tc + r]), _tile_at(buf_b, r), sem).start(priority=1)
        return carry

    lax.fori_loop(0, tc, issue, 0, unroll=8)
    pltpu.make_async_copy(ys_hbm.at[pl.ds(0, tc * SUBLANES)], buf_a, sem).wait()
    pltpu.make_async_copy(ys_hbm.at[pl.ds(0, tc * SUBLANES)], buf_b, sem).wait()
    top = top_ref[...]
    y = x_ref[...] + top[:, 2:3] * _from_tiles(buf_a) + top[:, 3:4] * _from_tiles(buf_b)
    o_ref[...] = _rms(y, gfin_ref[...]) if final else y


def _moe_combine(slot1, slot2, x, top, gfin, ys, final, tc):
    m, d = x.shape
    rows = lambda w: pl.BlockSpec((tc, w), lambda i, *_: (i, 0))
    return pl.pallas_call(
        functools.partial(_moe_combine_kernel, tc=tc, final=final),
        grid_spec=pltpu.PrefetchScalarGridSpec(
            num_scalar_prefetch=2,
            grid=(m // tc,),
            in_specs=[rows(d), rows(LANES), pl.BlockSpec((1, d), lambda i, *_: (0, 0)),
                      pl.BlockSpec(memory_space=pl.ANY)],
            out_specs=rows(d),
            scratch_shapes=[pltpu.VMEM((tc * SUBLANES, LANES), F32), pltpu.VMEM((tc * SUBLANES, LANES), F32),
                            pltpu.SemaphoreType.DMA(())]),
        out_shape=jax.ShapeDtypeStruct((m, d), F32),
        compiler_params=_params("arbitrary"),
        name="moe_combine",
    )(slot1, slot2, x, top, gfin, ys)


def _moe_routing(tops, tm):
    sizes = [t.shape[0] for t in tops]
    experts = jnp.concatenate([t[:, k] for k in range(2) for t in tops]).astype(jnp.int32)
    onehot = (experts[:, None] == jnp.arange(N_EXPERTS, dtype=jnp.int32)[None, :]).astype(jnp.int32)
    rank = jnp.sum((jnp.cumsum(onehot, axis=0) - onehot) * onehot, axis=1)
    counts = jnp.sum(onehot, axis=0)
    padded = ((counts + tm - 1) // tm) * tm
    ends = jnp.cumsum(padded)
    slot = ((ends - padded)[experts] + rank).astype(jnp.int32)
    n_tiles = -(-2 * sum(sizes) // tm) + N_EXPERTS
    tile_start = jnp.arange(n_tiles, dtype=jnp.int32) * tm
    tile_expert = jnp.minimum(jnp.sum((tile_start[:, None] >= ends[None, :]).astype(jnp.int32), axis=1),
                              N_EXPERTS - 1).astype(jnp.int32)
    n_valid = (ends[-1] // tm).astype(jnp.int32).reshape(1)
    offsets = [0]
    for s in sizes + sizes:
        offsets.append(offsets[-1] + s)
    pieces = [slot[offsets[j]:offsets[j + 1]] for j in range(2 * len(sizes))]
    slots = [(pieces[g], pieces[len(sizes) + g]) for g in range(len(sizes))]
    return slots, tile_expert, n_valid, n_tiles * tm


def _final_norm_kernel(x_ref, g_ref, o_ref):
    o_ref[...] = _rms(x_ref[...], g_ref[...])


def _final_norm(x, g, tm):
    m, d = x.shape
    row = pl.BlockSpec((tm, d), lambda i: (i, 0))
    return pl.pallas_call(
        _final_norm_kernel, grid=(m // tm,), in_specs=[row, _full(g.shape)], out_specs=row,
        out_shape=jax.ShapeDtypeStruct((m, d), F32), compiler_params=_params("parallel"),
        name="final_norm",
    )(x, g)


def _row_tile(m, want):
    t = min(m, want)
    assert m % t == 0, (m, t)
    return t


def kernel(x_prompt, x_sample, cache_k, cache_v, cache_logf, cache_conv, cache_mem_k, cache_mem_v, page_table, mem_prompt, g_mix, w_in, b_fgate, w_dw, b_dw, g_cln, b_cln, g_attn_out, g_conv_out, w_out, g_cross, g_mem, w_cq, w_ckv, w_co, g_ffn, w_ff_gu, w_ff_down, w_router, b_router, w_e_gu, w_e_down, g_final):
    nb, seq, d = x_prompt.shape
    nd, t_new, _ = x_sample.shape
    depth = w_in.shape[0]
    a = ATTN_WIDTH
    cw = d - a
    n_mem = mem_prompt.shape[1]
    n_phys, page = cache_k.shape[1], cache_k.shape[2]
    hist = CONV_K - 1
    vec = lambda v: v.reshape(1, -1).astype(F32)
    pad_lanes = lambda v, fill=0.0: jnp.pad(v, [(0, 0)] * (v.ndim - 1) + [(0, LANES - v.shape[-1])],
                                            constant_values=fill)

    xs = [x_prompt.reshape(nb * seq, d), x_sample.reshape(nd * t_new, d)]
    mem2 = mem_prompt.reshape(nb * n_mem, d)
    cache_kt = cache_k.transpose(0, 1, 3, 4, 2).reshape(depth * n_phys, a, page)
    cache_vt = cache_v.transpose(0, 1, 3, 4, 2).reshape(depth * n_phys, a, page)
    cache_lft = cache_logf.astype(F32).transpose(0, 1, 3, 2).reshape(depth * n_phys, N_HEADS, page)
    mem_k_rows = cache_mem_k.astype(F32).reshape(depth * nd, n_mem * X_HEADS, X_HEAD_DIM)
    mem_v_rows = cache_mem_v.astype(F32).reshape(depth * nd, n_mem * X_HEADS, X_HEAD_DIM)
    outs = {k: [] for k in ("lfp", "cvp", "mkp", "mvp", "ks", "vs", "lfs", "cvs")}
    kv_stack = (None, None)

    for l in range(depth):
        wqkv = w_in[l][:, :3 * a].astype(BF16)
        wf = pad_lanes(w_in[l][:, 3 * a:3 * a + N_HEADS]).astype(BF16)
        wglu = w_in[l][:, 3 * a + N_HEADS:].astype(BF16)
        bf = pad_lanes(vec(b_fgate[l]))
        wa = w_out[l][:a].astype(BF16)
        wc = w_out[l][a:].astype(BF16)
        wq = w_cq[l].astype(BF16)
        wo = w_co[l].astype(BF16)
        wk_mem = w_ckv[l][:, :X_WIDTH].astype(BF16)
        wv_mem = w_ckv[l][:, X_WIDTH:].astype(BF16)
        conv_args = (w_dw[l].astype(F32), vec(b_dw[l]), vec(g_cln[l]), vec(b_cln[l]), vec(g_conv_out[l]))
        last = l == depth - 1

        mk_p, mv_p = _memory_kv(mem2, vec(g_mem[l]), wk_mem, wv_mem, _row_tile(nb * n_mem, 512))
        outs["mkp"].append(mk_p.reshape(nb, n_mem, X_HEADS, X_HEAD_DIM))
        outs["mvp"].append(mv_p.reshape(nb, n_mem, X_HEADS, X_HEAD_DIM))
        mem_rows = n_mem * X_HEADS
        mems = [(mk_p.reshape(nb, mem_rows, X_HEAD_DIM), mv_p.reshape(nb, mem_rows, X_HEAD_DIM), 0),
                (mem_k_rows, mem_v_rows, l * nd)]

        tiles = [_row_tile(x.shape[0], 512) for x in xs]
        g_f = vec(g_ffn[l])
        router = None
        if l % 2 == 1:
            router = (g_f, pad_lanes(w_router[l // 2].astype(F32)), pad_lanes(vec(b_router[l // 2])))
        tops = [None, None]
        for grp in range(2):
            x = xs[grp]
            m = x.shape[0]
            tm = tiles[grp]
            stacked = (seq, kv_stack[0], kv_stack[1]) if grp == 0 else None
            q, k, v, kb, vb, lf_pad, u = _proj_in(x, vec(g_mix[l]), wqkv, wf, wglu, bf, tm, stacked)
            lf = lf_pad[:, :N_HEADS]
            if grp == 0:
                c = _cumsum_logf(lf_pad, nb, seq)
                o_attn = _fox_prompt(q, kb, vb, c, nb, seq, _row_tile(seq, 512), _row_tile(seq, 512))
                u3 = u.reshape(nb, seq, cw)
                o_conv = _conv_prompt(u3, *conv_args, _row_tile(seq, 512)).reshape(m, cw)
                kv_stack = (k, v)
                outs["lfp"].append(lf.reshape(nb, seq, N_HEADS))
                outs["cvp"].append(u3[:, seq - hist:])
                n_grp, t_grp = nb, seq
            else:
                new_t = lambda z: jnp.pad(z.reshape(nd, t_new, -1).transpose(0, 2, 1),
                                          ((0, 0), (0, 0), (0, page - t_new)))
                o_attn = _fox_sample(page_table, l * n_phys, q, new_t(k).reshape(nd * a, page),
                                     new_t(v).reshape(nd * a, page), new_t(lf),
                                     cache_kt, cache_vt, cache_lft, t_new)
                u_hist = jnp.concatenate([cache_conv[l].astype(F32), u.reshape(nd, t_new, cw)], axis=1)
                o_conv = _conv_sample(u_hist, *conv_args, t_new).reshape(m, cw)
                outs["ks"].append(k.reshape(nd, t_new, N_HEADS, HEAD_DIM))
                outs["vs"].append(v.reshape(nd, t_new, N_HEADS, HEAD_DIM))
                outs["lfs"].append(lf.reshape(nd, t_new, N_HEADS))
                outs["cvs"].append(u_hist[:, -hist:])
                n_grp, t_grp = nd, t_new
            xs[grp], tops[grp] = _mix_cross(x, o_attn, o_conv, vec(g_attn_out[l]), wa, wc, vec(g_cross[l]), wq,
                                            mems[grp][0], mems[grp][1], wo, n_grp, mems[grp][2],
                                            _row_tile(t_grp, 512), router)

        if l % 2 == 0:
            wgu_d, wd_d = w_ff_gu[l // 2].astype(BF16), w_ff_down[l // 2].astype(BF16)
            xs = [_ffn(x, g_f, wgu_d, wd_d, tm, 256) for x, tm in zip(xs, tiles)]
            if last:
                xs = [_final_norm(x, vec(g_final), tm) for x, tm in zip(xs, tiles)]
        else:
            e = l // 2
            slots, tile_expert, n_valid, n_slots = _moe_routing(tops, MOE_TILE)
            slot_rows = jnp.zeros((n_slots * SUBLANES, LANES), F32)
            for x, (s1, s2) in zip(xs, slots):
                slot_rows = _moe_dispatch(s1, s2, x, g_f, slot_rows, _row_tile(x.shape[0], 256))
            ys = _moe_sparse(tile_expert, n_valid, slot_rows, w_e_gu[e], w_e_down[e], MOE_TILE, 512)
            xs = [_moe_combine(s1, s2, x, top, vec(g_final), ys, last, _row_tile(x.shape[0], 256))
                  for x, top, (s1, s2) in zip(xs, tops, slots)]

    st = lambda key: jnp.stack(outs[key])
    kv_prompt = lambda z: z.reshape(depth, nb, N_HEADS, HEAD_DIM, seq).transpose(0, 1, 4, 2, 3)
    return (xs[0].reshape(nb, seq, d), xs[1].reshape(nd, t_new, d),
            kv_prompt(kv_stack[0]), kv_prompt(kv_stack[1]), st("lfp"), st("cvp"), st("mkp"), st("mvp"),
            st("ks"), st("vs"), st("lfs"), st("cvs"))
```

```python
import functools

import jax
import jax.numpy as jnp
from jax import lax
from jax.experimental import pallas as pl
from jax.experimental.pallas import tpu as pltpu

EPS = 1e-6
N_HEADS = 8
HEAD_DIM = 64
ATTN_WIDTH = N_HEADS * HEAD_DIM
CONV_K = 31
CONV_HALO = 32
X_HEADS = 4
X_HEAD_DIM = 128
X_WIDTH = X_HEADS * X_HEAD_DIM
N_EXPERTS = 8
LANES = 128
SUBLANES = 8
NEG = -1e30
LOG2E = 1.4426950408889634
VMEM_LIMIT_BYTES = 56 * 1024 * 1024
PAGES_PER_STEP = 16
MOE_TILE = 1024

F32 = jnp.float32
BF16 = jnp.bfloat16
HIGHEST = lax.Precision.HIGHEST


def _params(*sem):
    return pltpu.CompilerParams(dimension_semantics=sem, vmem_limit_bytes=VMEM_LIMIT_BYTES)


def _rms(x, g):
    return x * lax.rsqrt(jnp.mean(x * x, axis=-1, keepdims=True) + EPS) * g


def _sigmoid(x):
    return 1.0 / (1.0 + jnp.exp(-x))


def _dot(a, b):
    return jnp.dot(a, b, preferred_element_type=F32)


def _dot_nt(a, b):
    return lax.dot_general(a, b, (((1,), (1,)), ((), ())), preferred_element_type=F32)


def _full(shape):
    nd = len(shape)
    return pl.BlockSpec(shape, lambda *_: (0,) * nd)


def _proj_in_kernel(x_ref, g_ref, wqkv_ref, wf_ref, wglu_ref, bf_ref,
                    q_ref, k_ref, v_ref, kb_ref, vb_ref, lf_ref, u_ref, *, transposed):
    a = ATTN_WIDTH
    h = _rms(x_ref[...], g_ref[...]).astype(BF16)
    qkv = _dot(h, wqkv_ref[...])
    q_ref[...] = (qkv[:, :a] * (HEAD_DIM ** -0.5 * LOG2E)).astype(BF16)
    k = qkv[:, a:2 * a]
    v = qkv[:, 2 * a:]
    if transposed:
        k_ref[0, 0] = k.T
        v_ref[0, 0] = v.T
    else:
        k_ref[...] = k
        v_ref[...] = v
    kb_ref[...] = k.astype(BF16)
    vb_ref[...] = v.astype(BF16)
    z = _dot(h, wf_ref[...]) + bf_ref[...]
    lf_ref[...] = jnp.minimum(z, 0.0) - jnp.log1p(jnp.exp(-jnp.abs(z)))
    glu = _dot(h, wglu_ref[...])
    c = glu.shape[1] // 2
    u_ref[...] = glu[:, :c] * _sigmoid(glu[:, c:])


def _proj_in_carry_kernel(inner, x_ref, g_ref, wqkv_ref, wf_ref, wglu_ref, bf_ref, k_prev, v_prev,
                          q_ref, k_ref, v_ref, *out_refs):
    n_prev = k_prev.shape[0]
    k_ref[0:n_prev] = k_prev[...]
    v_ref[0:n_prev] = v_prev[...]
    inner(x_ref, g_ref, wqkv_ref, wf_ref, wglu_ref, bf_ref, q_ref, k_ref.at[n_prev:], v_ref.at[n_prev:], *out_refs)


def _proj_in(x, g, wqkv, wf, wglu, bf, tm, stacked=None):
    m, d = x.shape
    a = ATTN_WIDTH
    c = wglu.shape[1] // 2
    row = lambda w: pl.BlockSpec((tm, w), lambda i: (i, 0))
    args = [x, g, wqkv, wf, wglu, bf]
    in_specs = [row(d), _full((1, d)), _full(wqkv.shape), _full(wf.shape), _full(wglu.shape), _full((1, LANES))]
    if stacked is None:
        kv_spec, kv_shape = row(a), jax.ShapeDtypeStruct((m, a), F32)
        kernel_fn = functools.partial(_proj_in_kernel, transposed=False)
    else:
        seq_len, k_prev, v_prev = stacked
        per = seq_len // tm
        n_prev = 0 if k_prev is None else k_prev.shape[0]
        slabs = lambda n: pl.BlockSpec((n, 1, a, tm), lambda i: (0, i // per, 0, i % per))
        kv_spec = slabs(n_prev + 1)
        kv_shape = jax.ShapeDtypeStruct((n_prev + 1, m // seq_len, a, seq_len), F32)
        kernel_fn = functools.partial(_proj_in_kernel, transposed=True)
        if n_prev:
            args += [k_prev, v_prev]
            in_specs += [slabs(n_prev)] * 2
            kernel_fn = functools.partial(_proj_in_carry_kernel, kernel_fn)
    return pl.pallas_call(
        kernel_fn,
        grid=(m // tm,),
        in_specs=in_specs,
        out_specs=[row(a), kv_spec, kv_spec, row(a), row(a), row(LANES), row(c)],
        out_shape=[jax.ShapeDtypeStruct((m, a), BF16), kv_shape, kv_shape, jax.ShapeDtypeStruct((m, a), BF16),
                   jax.ShapeDtypeStruct((m, a), BF16), jax.ShapeDtypeStruct((m, LANES), F32),
                   jax.ShapeDtypeStruct((m, c), F32)],
        compiler_params=_params("parallel"),
        name="proj_in",
    )(*args)


def _cumsum_kernel(lf_ref, c_ref):
    s = lf_ref.shape[0]
    lft = lf_ref[...].T[:N_HEADS, :]
    r = lax.broadcasted_iota(jnp.int32, (LANES, LANES), 0)
    cidx = lax.broadcasted_iota(jnp.int32, (LANES, LANES), 1)
    tri = (r <= cidx).astype(F32)
    carry = jnp.zeros((N_HEADS, 1), F32)
    for b in range(s // LANES):
        cb = jnp.dot(lft[:, b * LANES:(b + 1) * LANES], tri, precision=HIGHEST,
                     preferred_element_type=F32) + carry
        c_ref[0, :, b * LANES:(b + 1) * LANES] = cb * LOG2E
        carry = cb[:, LANES - 1:LANES]


def _cumsum_logf(lf_pad, nb, s):
    return pl.pallas_call(
        _cumsum_kernel,
        grid=(nb,),
        in_specs=[pl.BlockSpec((s, LANES), lambda b: (b, 0))],
        out_specs=pl.BlockSpec((1, N_HEADS, s), lambda b: (b, 0, 0)),
        out_shape=jax.ShapeDtypeStruct((nb, N_HEADS, s), F32),
        compiler_params=_params("parallel"),
        name="cumsum_logf",
    )(lf_pad)


def _fox_prompt_kernel(q_ref, k_ref, v_ref, c_ref, o_ref, m_scr, l_scr, acc_scr, *, tq, tk):
    i = pl.program_id(2)
    lane = lax.broadcasted_iota(jnp.int32, (1, LANES), 1)
    q = q_ref[...]
    zero = jnp.zeros_like(q)
    qh = (jnp.where(lane < HEAD_DIM, q, zero), jnp.where(lane >= HEAD_DIM, q, zero))
    m_scr[...] = jnp.full(m_scr.shape, NEG, F32)
    l_scr[...] = jnp.zeros(l_scr.shape, F32)
    acc_scr[...] = jnp.zeros(acc_scr.shape, F32)
    row = lax.broadcasted_iota(jnp.int32, (tq, tk), 0)
    col = lax.broadcasted_iota(jnp.int32, (tq, tk), 1)
    ratio = tq // tk

    def step(j, diag):
        start = pl.multiple_of(j * tk, tk)
        kb = k_ref[pl.ds(start, tk), :]
        vb = v_ref[pl.ds(start, tk), :]
        cs = c_ref[:, j]
        for hh in range(2):
            s = _dot_nt(qh[hh], kb) - cs[hh]
            if diag is not None:
                s = jnp.where(col + diag * tk <= row, s, NEG)
            m_prev = m_scr[hh]
            m_new = jnp.maximum(m_prev, jnp.max(s, axis=-1, keepdims=True))
            alpha = jnp.exp2(m_prev - m_new)
            p = jnp.exp2(s - jnp.concatenate([m_new] * (tk // LANES), axis=1))
            l_scr[hh] = alpha * l_scr[hh] + jnp.sum(p, axis=-1, keepdims=True)
            acc_scr[hh] = alpha * acc_scr[hh] + _dot(p.astype(BF16), vb)
            m_scr[hh] = m_new

    def body(j, carry):
        step(j, None)
        return carry

    lax.fori_loop(0, i * ratio, body, 0)
    for dg in range(ratio):
        step(i * ratio + dg, dg)
    o_ref[...] = jnp.where(lane < HEAD_DIM, acc_scr[0] / l_scr[0], acc_scr[1] / l_scr[1])


def _fox_prompt(q, kb, vb, c, nb, s, tq, tk):
    m = q.shape[0]
    nt = s // tq
    nk = s // tk
    c4 = c.reshape(nb * N_HEADS, nk, 1, tk)
    pairs = N_HEADS // 2
    return pl.pallas_call(
        functools.partial(_fox_prompt_kernel, tq=tq, tk=tk),
        grid=(nb, pairs, nt),
        in_specs=[pl.BlockSpec((tq, LANES), lambda b, p, i: (b * nt + i, p)),
                  pl.BlockSpec((s, LANES), lambda b, p, i: (b, p)),
                  pl.BlockSpec((s, LANES), lambda b, p, i: (b, p)),
                  pl.BlockSpec((2, nk, 1, tk), lambda b, p, i: (b * pairs + p, 0, 0, 0))],
        out_specs=pl.BlockSpec((tq, LANES), lambda b, p, i: (b * nt + i, p)),
        out_shape=jax.ShapeDtypeStruct((m, ATTN_WIDTH), F32),
        scratch_shapes=[pltpu.VMEM((2, tq, LANES), F32), pltpu.VMEM((2, tq, LANES), F32),
                        pltpu.VMEM((2, tq, LANES), F32)],
        compiler_params=_params("parallel", "parallel", "arbitrary"),
        name="fox_prompt",
    )(q, kb, vb, c4)


def _fox_sample_kernel(pt_ref, q_ref, kn_ref, vn_ref, lfn_ref, *rest, t_new, n_chunks):
    pg = PAGES_PER_STEP
    k_refs, v_refs, lf_refs = rest[:pg], rest[pg:2 * pg], rest[2 * pg:3 * pg]
    o_ref, m_scr, l_scr, acc_scr, carry_scr = rest[3 * pg:]
    del pt_ref
    j = pl.program_id(1)
    rows = N_HEADS * t_new
    width = ATTN_WIDTH
    page = lf_refs[0].shape[2]

    @pl.when(j == 0)
    def _():
        m_scr[...] = jnp.full(m_scr.shape, NEG, F32)
        l_scr[...] = jnp.zeros(l_scr.shape, F32)
        acc_scr[...] = jnp.zeros(acc_scr.shape, F32)
        carry_scr[...] = jnp.zeros(carry_scr.shape, F32)

    q = q_ref[...]
    q_rows = jnp.concatenate([q] * N_HEADS, axis=0)
    rr = lax.broadcasted_iota(jnp.int32, (rows, width), 0)
    ll = lax.broadcasted_iota(jnp.int32, (rows, width), 1)
    head_mask = (rr // t_new) == (ll // HEAD_DIM)
    q_rows = jnp.where(head_mask, q_rows, jnp.zeros_like(q_rows))

    def per_query_rows(x):
        return jnp.concatenate([jnp.broadcast_to(x[h:h + 1], (t_new, x.shape[1])) for h in range(N_HEADS)],
                               axis=0)

    def attend(kts, vts, biases):
        s = jnp.concatenate([_dot(q_rows, kt) + b for kt, b in zip(kts, biases)], axis=1)
        m_prev = m_scr[...]
        m_new = jnp.maximum(m_prev, jnp.max(s, axis=-1, keepdims=True))
        alpha = jnp.exp2(m_prev - m_new)
        p = jnp.exp2(s - m_new)
        l_scr[...] = alpha * l_scr[...] + jnp.sum(p, axis=-1, keepdims=True)
        p = p.astype(BF16)
        pv = _dot_nt(p[:, :page], vts[0])
        for g in range(1, len(vts)):
            pv = pv + _dot_nt(p[:, g * page:(g + 1) * page], vts[g])
        acc_scr[...] = alpha * acc_scr[...] + pv
        m_scr[...] = m_new

    lf = jnp.concatenate([lf_refs[g][0] for g in range(pg)], axis=0)
    lane_all = lax.broadcasted_iota(jnp.int32, lf.shape, 1)
    suffix = lf
    shift = 1
    while shift < page:
        nxt = pltpu.roll(suffix, page - shift, axis=1)
        suffix = suffix + jnp.where(lane_all + shift < page, nxt, 0.0)
        shift *= 2
    within = suffix - lf
    totals = jnp.sum(lf, axis=-1, keepdims=True)
    later = carry_scr[...]
    biases = [None] * pg
    for g in reversed(range(pg)):
        biases[g] = per_query_rows((within[g * N_HEADS:(g + 1) * N_HEADS] + later) * LOG2E)
        later = later + totals[g * N_HEADS:(g + 1) * N_HEADS]
    carry_scr[...] = later
    attend([k_refs[g][0].astype(BF16) for g in range(pg)], [v_refs[g][0].astype(BF16) for g in range(pg)], biases)
    lane = lax.broadcasted_iota(jnp.int32, (N_HEADS, page), 1)

    @pl.when(j == n_chunks - 1)
    def _():
        c_new = lfn_ref[0]
        shift = 1
        while shift < t_new:
            c_new = c_new + jnp.where(lane >= shift, pltpu.roll(c_new, shift, axis=1), 0.0)
            shift *= 2
        br = lax.broadcasted_iota(jnp.int32, (rows, page), 0)
        bc = lax.broadcasted_iota(jnp.int32, (rows, page), 1)
        bias = jnp.where(bc <= (br % t_new), per_query_rows(c_new * (-LOG2E)), NEG)
        attend([kn_ref[...].astype(BF16)], [vn_ref[...].astype(BF16)], [bias])
        o = jnp.where(head_mask, acc_scr[...] / l_scr[...], 0.0)
        out = o[0:t_new]
        for h in range(1, N_HEADS):
            out = out + o[h * t_new:(h + 1) * t_new]
        o_ref[...] = out


def _fox_sample(page_table, base, q, kt_new, vt_new, lft_new, cache_kt, cache_vt, cache_lft, t_new):
    n, n_pages = page_table.shape
    pg = PAGES_PER_STEP
    n_chunks = n_pages // pg
    assert n_pages % pg == 0
    width, page = cache_kt.shape[1], cache_kt.shape[2]
    rows = N_HEADS * t_new

    def page_spec(shape, g):
        return pl.BlockSpec(shape, lambda b, j, pt: (base + pt[b, (n_chunks - 1 - j) * pg + g], 0, 0))

    in_specs = ([pl.BlockSpec((t_new, width), lambda b, j, pt: (b, 0)),
                 pl.BlockSpec((width, page), lambda b, j, pt: (b, 0)),
                 pl.BlockSpec((width, page), lambda b, j, pt: (b, 0)),
                 pl.BlockSpec((1, N_HEADS, page), lambda b, j, pt: (b, 0, 0))]
                + [page_spec((1, width, page), g) for g in range(pg)]
                + [page_spec((1, width, page), g) for g in range(pg)]
                + [page_spec((1, N_HEADS, page), g) for g in range(pg)])
    return pl.pallas_call(
        functools.partial(_fox_sample_kernel, t_new=t_new, n_chunks=n_chunks),
        grid_spec=pltpu.PrefetchScalarGridSpec(
            num_scalar_prefetch=1,
            grid=(n, n_chunks),
            in_specs=in_specs,
            out_specs=pl.BlockSpec((t_new, width), lambda b, j, pt: (b, 0)),
            scratch_shapes=[pltpu.VMEM((rows, 1), F32), pltpu.VMEM((rows, 1), F32),
                            pltpu.VMEM((rows, width), F32), pltpu.VMEM((N_HEADS, 1), F32)]),
        out_shape=jax.ShapeDtypeStruct((n * t_new, width), F32),
        compiler_params=_params("parallel", "arbitrary"),
        name="fox_sample",
    )(page_table, q, kt_new, vt_new, lft_new, *([cache_kt] * pg), *([cache_vt] * pg), *([cache_lft] * pg))


def _conv_tail(z, b_ref, gl_ref, bl_ref, gco_ref):
    z = z + b_ref[...]
    zc = z - jnp.mean(z, axis=-1, keepdims=True)
    y = zc * lax.rsqrt(jnp.mean(zc * zc, axis=-1, keepdims=True) + EPS) * gl_ref[...] + bl_ref[...]
    y = y * _sigmoid(y)
    return _rms(y, gco_ref[...]).astype(BF16)


def _conv_prompt_kernel(u_ref, halo_ref, w_ref, b_ref, gl_ref, bl_ref, gco_ref, o_ref, xs_ref, sh_ref, *, ts, rc):
    i = pl.program_id(1)
    halo = halo_ref[0]
    xs_ref[0:CONV_HALO, :] = jnp.where(i > 0, halo, jnp.zeros_like(halo))
    xs_ref[CONV_HALO:CONV_HALO + ts, :] = u_ref[0]
    n_sh = sh_ref.shape[1]
    for r in range(1, SUBLANES):
        sh_ref[r - 1] = xs_ref[r:r + n_sh, :]
    off = CONV_HALO - (CONV_K - 1)
    width = u_ref.shape[2]
    for c in range(ts // rc):
        acc = jnp.zeros((rc // SUBLANES, SUBLANES, width), F32)
        for j in range(CONV_K):
            q, r = divmod(off + j, SUBLANES)
            lo = c * rc + q * SUBLANES
            rows = xs_ref[lo:lo + rc, :] if r == 0 else sh_ref[r - 1, lo:lo + rc, :]
            acc = acc + w_ref[j][None] * rows.reshape(rc // SUBLANES, SUBLANES, width)
        o_ref[0, c * rc:(c + 1) * rc, :] = _conv_tail(acc.reshape(rc, width), b_ref, gl_ref, bl_ref, gco_ref)


def _conv_prompt(u3, w, b, gl, bl, gco, ts):
    nb, s, c = u3.shape
    hb = ts // CONV_HALO
    vec = _full((1, c))
    w = jnp.broadcast_to(w[:, None, :], (w.shape[0], SUBLANES, c))
    return pl.pallas_call(
        functools.partial(_conv_prompt_kernel, ts=ts, rc=32),
        grid=(nb, s // ts),
        in_specs=[pl.BlockSpec((1, ts, c), lambda n, i: (n, i, 0)),
                  pl.BlockSpec((1, CONV_HALO, c), lambda n, i: (n, jnp.maximum(i * hb - 1, 0), 0)),
                  _full(w.shape), vec, vec, vec, vec],
        out_specs=pl.BlockSpec((1, ts, c), lambda n, i: (n, i, 0)),
        out_shape=jax.ShapeDtypeStruct((nb, s, c), BF16),
        scratch_shapes=[pltpu.VMEM((CONV_HALO + ts, c), F32),
                        pltpu.VMEM((SUBLANES - 1, CONV_HALO + ts - SUBLANES, c), F32)],
        compiler_params=_params("parallel", "parallel"),
        name="conv_prompt",
    )(u3, u3, w, b, gl, bl, gco)


def _conv_sample_kernel(uh_ref, w_ref, b_ref, gl_ref, bl_ref, gco_ref, o_ref, *, t_new):
    def one_sequence(n, carry):
        acc = jnp.zeros((t_new, uh_ref.shape[2]), F32)
        for j in range(CONV_K):
            acc = acc + w_ref[j:j + 1, :] * uh_ref[n, j:j + t_new, :]
        o_ref[n] = _conv_tail(acc, b_ref, gl_ref, bl_ref, gco_ref)
        return carry

    lax.fori_loop(0, uh_ref.shape[0], one_sequence, 0)


def _conv_sample(u_hist, w, b, gl, bl, gco, t_new):
    n, hl, c = u_hist.shape
    vec = _full((1, c))
    return pl.pallas_call(
        functools.partial(_conv_sample_kernel, t_new=t_new),
        grid=(1,),
        in_specs=[_full(u_hist.shape), _full(w.shape), vec, vec, vec, vec],
        out_specs=_full((n, t_new, c)),
        out_shape=jax.ShapeDtypeStruct((n, t_new, c), BF16),
        compiler_params=_params("arbitrary"),
        name="conv_sample",
    )(u_hist, w, b, gl, bl, gco)


def _memory_kv_kernel(mem_ref, g_ref, wk_ref, wv_ref, mk_ref, mv_ref):
    h = _rms(mem_ref[...], g_ref[...]).astype(BF16)
    rows = mem_ref.shape[0]
    for ref, w_ref in ((mk_ref, wk_ref), (mv_ref, wv_ref)):
        kv = _dot(h, w_ref[...])
        for hh in range(X_HEADS):
            ref[pl.ds(hh, rows, stride=X_HEADS), :] = kv[:, hh * X_HEAD_DIM:(hh + 1) * X_HEAD_DIM]


def _memory_kv(mem, g, wk, wv, tm):
    m, d = mem.shape
    out = pl.BlockSpec((tm * X_HEADS, X_HEAD_DIM), lambda i: (i, 0))
    return pl.pallas_call(
        _memory_kv_kernel,
        grid=(m // tm,),
        in_specs=[pl.BlockSpec((tm, d), lambda i: (i, 0)), _full(g.shape), _full(wk.shape), _full(wv.shape)],
        out_specs=[out, out],
        out_shape=[jax.ShapeDtypeStruct((m * X_HEADS, X_HEAD_DIM), F32)] * 2,
        compiler_params=_params("parallel"),
        name="memory_kv",
    )(mem, g, wk, wv)


def _mix_cross_kernel(x_ref, oa_ref, oc_ref, ga_ref, wa_ref, wc_ref, g_ref, wq_ref, mk_ref, mv_ref, wo_ref,
                      *rest, routed):
    o_ref = rest[4] if routed else rest[0]
    a = _rms(oa_ref[...], ga_ref[...]).astype(BF16)
    x = x_ref[...] + _dot(a, wa_ref[...]) + _dot(oc_ref[...], wc_ref[...])
    h = _rms(x, g_ref[...]).astype(BF16)
    q = (_dot(h, wq_ref[...]) * (X_HEAD_DIM ** -0.5)).astype(BF16)
    n_mem = mk_ref.shape[1] // X_HEADS
    outs = []
    for hh in range(X_HEADS):
        head = pl.ds(hh, n_mem, stride=X_HEADS)
        s = _dot_nt(q[:, hh * X_HEAD_DIM:(hh + 1) * X_HEAD_DIM], mk_ref[0, head, :].astype(BF16))
        p = jnp.exp(s - jnp.max(s, axis=-1, keepdims=True))
        o = _dot(p.astype(BF16), mv_ref[0, head, :].astype(BF16)) / jnp.sum(p, axis=-1, keepdims=True)
        outs.append(o.astype(BF16))
    x = x + _dot(jnp.concatenate(outs, axis=1), wo_ref[...])
    o_ref[...] = x
    if routed:
        gf_ref, whi_ref, wlo_ref, br_ref, _, top_ref = rest
        top_ref[...] = _route(x, gf_ref[...], (whi_ref[...], wlo_ref[...]), br_ref[...])


def _mix_cross(x, o_attn, o_conv, ga, wa, wc, g, wq, mk, mv, wo, nb, base, tm, router=None):
    m, d = x.shape
    per = (m // nb) // tm
    mem_rows = mk.shape[1]
    rows = lambda w: pl.BlockSpec((tm, w), lambda b, i: (b * per + i, 0))
    mem = pl.BlockSpec((1, mem_rows, X_HEAD_DIM), lambda b, i: (base + b, 0, 0))
    args = [x, o_attn, o_conv, ga, wa, wc, g, wq, mk, mv, wo]
    in_specs = [rows(d), rows(o_attn.shape[1]), rows(o_conv.shape[1]), _full(ga.shape), _full(wa.shape),
                _full(wc.shape), _full(g.shape), _full(wq.shape), mem, mem, _full(wo.shape)]
    out_specs, out_shape = [rows(d)], [jax.ShapeDtypeStruct((m, d), F32)]
    if router is not None:
        args += list(router)
        in_specs += [_full(r.shape) for r in router]
        out_specs.append(rows(LANES))
        out_shape.append(jax.ShapeDtypeStruct((m, LANES), F32))
    res = pl.pallas_call(
        functools.partial(_mix_cross_kernel, routed=router is not None),
        grid=(nb, per),
        in_specs=in_specs,
        out_specs=out_specs,
        out_shape=out_shape,
        compiler_params=_params("parallel", "parallel"),
        name="mix_cross",
    )(*args)
    return (res[0], res[1]) if router is not None else (res[0], None)


def _ffn_kernel(x_ref, g_ref, wgu_ref, wd_ref, o_ref, acc_ref, *, tf):
    x = x_ref[...]
    h = _rms(x, g_ref[...]).astype(BF16)
    f = wd_ref.shape[0]
    acc_ref[...] = x
    for c in range(f // tf):
        gt = _dot(h, wgu_ref[:, c * tf:(c + 1) * tf])
        up = _dot(h, wgu_ref[:, f + c * tf:f + (c + 1) * tf])
        act = (gt * _sigmoid(gt) * up).astype(BF16)
        acc_ref[...] += _dot(act, wd_ref[c * tf:(c + 1) * tf, :])
    o_ref[...] = acc_ref[...]


def _ffn(x, g, wgu, wd, tm, tf):
    m, d = x.shape
    row = pl.BlockSpec((tm, d), lambda i: (i, 0))
    resident = lambda shape: pl.BlockSpec(shape, lambda i: (0, 0), pipeline_mode=pl.Buffered(1))
    return pl.pallas_call(
        functools.partial(_ffn_kernel, tf=tf),
        grid=(m // tm,),
        in_specs=[row, _full(g.shape), resident(wgu.shape), resident(wd.shape)],
        out_specs=row,
        out_shape=jax.ShapeDtypeStruct((m, d), F32),
        scratch_shapes=[pltpu.VMEM((tm, d), F32)],
        compiler_params=_params("parallel"),
        name="ffn_dense",
    )(x, g, wgu, wd)


def _route(x, g, wr, br):
    h = _rms(x, g)
    h_hi = h.astype(BF16)
    h_lo = (h - h_hi.astype(F32)).astype(BF16)
    w_hi, w_lo = wr
    logits = _dot(h_hi, w_hi) + _dot(h_hi, w_lo) + _dot(h_lo, w_hi) + br
    lane = lax.broadcasted_iota(jnp.int32, logits.shape, 1)
    logits = jnp.where(lane < N_EXPERTS, logits, NEG)
    p = jnp.exp(logits - jnp.max(logits, axis=-1, keepdims=True))
    p = p / jnp.sum(p, axis=-1, keepdims=True)
    v1 = jnp.max(p, axis=-1, keepdims=True)
    i1 = jnp.min(jnp.where(p == v1, lane, LANES), axis=-1, keepdims=True)
    rest = jnp.where(lane == i1, -1.0, p)
    v2 = jnp.max(rest, axis=-1, keepdims=True)
    i2 = jnp.min(jnp.where(rest == v2, lane, LANES), axis=-1, keepdims=True)
    tot = v1 + v2
    g1 = v1 / tot
    g2 = v2 / tot
    return jnp.where(lane == 0, i1.astype(F32),
                     jnp.where(lane == 1, i2.astype(F32), jnp.where(lane == 2, g1, jnp.where(lane == 3, g2, 0.0))))


def _to_tiles(ref, x):
    for a in range(SUBLANES):
        ref[pl.ds(a, x.shape[0], stride=SUBLANES), :] = x[:, a * LANES:(a + 1) * LANES]


def _from_tiles(ref):
    rows = ref.shape[0] // SUBLANES
    return jnp.concatenate([ref[pl.ds(a, rows, stride=SUBLANES), :] for a in range(SUBLANES)], axis=1)


def _tile_at(ref, slot):
    return ref.at[pl.ds(pl.multiple_of(slot * SUBLANES, SUBLANES), SUBLANES)]


def _moe_dispatch_kernel(s1_ref, s2_ref, x_ref, g_ref, xs_in, xs_hbm, stage, sem, *, td):
    del xs_in
    i = pl.program_id(0)
    _to_tiles(stage, _rms(x_ref[...], g_ref[...]))

    def issue(r, carry):
        src = _tile_at(stage, r)
        pltpu.make_async_copy(src, _tile_at(xs_hbm, s1_ref[i * td + r]), sem).start(priority=0)
        pltpu.make_async_copy(src, _tile_at(xs_hbm, s2_ref[i * td + r]), sem).start(priority=1)
        return carry

    lax.fori_loop(0, td, issue, 0, unroll=8)
    for _ in range(2):
        pltpu.make_async_copy(stage, xs_hbm.at[pl.ds(0, td * SUBLANES)], sem).wait()


def _moe_dispatch(slot1, slot2, x, g, xs, td):
    m, d = x.shape
    return pl.pallas_call(
        functools.partial(_moe_dispatch_kernel, td=td),
        grid_spec=pltpu.PrefetchScalarGridSpec(
            num_scalar_prefetch=2,
            grid=(m // td,),
            in_specs=[pl.BlockSpec((td, d), lambda i, *_: (i, 0)), pl.BlockSpec((1, d), lambda i, *_: (0, 0)),
                      pl.BlockSpec(memory_space=pl.ANY)],
            out_specs=pl.BlockSpec(memory_space=pl.ANY),
            scratch_shapes=[pltpu.VMEM((td * SUBLANES, LANES), F32), pltpu.SemaphoreType.DMA(())]),
        out_shape=jax.ShapeDtypeStruct(xs.shape, F32),
        input_output_aliases={4: 0},
        compiler_params=_params("arbitrary"),
        name="moe_dispatch",
    )(slot1, slot2, x, g, xs)


def _moe_sparse_kernel(te_ref, rows_ref, x_ref, wg_ref, wu_ref, wd_ref, o_ref, h_scr, acc_scr):
    del te_ref
    i = pl.program_id(0)
    f = pl.program_id(1)
    last = pl.num_programs(1) - 1
    rows = rows_ref[i]
    tm = h_scr.shape[0]
    half = tm // 2

    @pl.when(f == 0)
    def _():
        acc_scr[...] = jnp.zeros(acc_scr.shape, F32)
        h_scr[...] = _from_tiles(x_ref).astype(BF16)

    def swiglu(n):
        h = h_scr[0:n]
        gt = _dot(h, wg_ref[0].astype(BF16))
        up = _dot(h, wu_ref[0].astype(BF16))
        act = (gt * _sigmoid(gt) * up).astype(BF16)
        acc_scr[0:n] += _dot(act, wd_ref[0].astype(BF16))

    @pl.when(rows > half)
    def _():
        swiglu(tm)

    @pl.when((rows > 0) & (rows <= half))
    def _():
        swiglu(half)

    @pl.when(f == last)
    def _():
        _to_tiles(o_ref, acc_scr[...])


def _moe_sparse(tile_expert, tile_rows, xs, wgu, wd, tm, tf):
    n_slots = xs.shape[0] // SUBLANES
    d, f_all = wd.shape[2], wd.shape[1]
    nf = f_all // tf

    def chunk(i, f, nv):
        return jnp.where(nv[i] > 0, f, nf - 1)

    tiles = pl.BlockSpec((tm * SUBLANES, LANES), lambda i, f, te, nv: (i, 0))
    return pl.pallas_call(
        _moe_sparse_kernel,
        grid_spec=pltpu.PrefetchScalarGridSpec(
            num_scalar_prefetch=2,
            grid=(n_slots // tm, nf),
            in_specs=[tiles,
                      pl.BlockSpec((1, d, tf), lambda i, f, te, nv: (te[i], 0, chunk(i, f, nv))),
                      pl.BlockSpec((1, d, tf), lambda i, f, te, nv: (te[i], 0, nf + chunk(i, f, nv))),
                      pl.BlockSpec((1, tf, d), lambda i, f, te, nv: (te[i], chunk(i, f, nv), 0))],
            out_specs=tiles,
            scratch_shapes=[pltpu.VMEM((tm, d), BF16), pltpu.VMEM((tm, d), F32)]),
        out_shape=jax.ShapeDtypeStruct(xs.shape, F32),
        compiler_params=_params("arbitrary", "arbitrary"),
        name="moe_sparse",
    )(tile_expert, tile_rows, xs, wgu, wgu, wd)


def _moe_combine_kernel(s1_ref, s2_ref, x_ref, top_ref, gfin_ref, ys_hbm, o_ref, buf_a, buf_b, sem, *, tc, final):
    i = pl.program_id(0)

    def issue(r, carry):
        pltpu.make_async_copy(_tile_at(ys_hbm, s1_ref[i * tc + r]), _tile_at(buf_a, r), sem).start(priority=0)
        pltpu.make_async_copy(_tile_at(ys_hbm, s2_ref[i * tc + r]), _tile_at(buf_b, r), sem).start(priority=1)
        return carry

    lax.fori_loop(0, tc, issue, 0, unroll=8)
    pltpu.make_async_copy(ys_hbm.at[pl.ds(0, tc * SUBLANES)], buf_a, sem).wait()
    pltpu.make_async_copy(ys_hbm.at[pl.ds(0, tc * SUBLANES)], buf_b, sem).wait()
    top = top_ref[...]
    y = x_ref[...] + top[:, 2:3] * _from_tiles(buf_a) + top[:, 3:4] * _from_tiles(buf_b)
    o_ref[...] = _rms(y, gfin_ref[...]) if final else y


def _moe_combine(slot1, slot2, x, top, gfin, ys, final, tc):
    m, d = x.shape
    rows = lambda w: pl.BlockSpec((tc, w), lambda i, *_: (i, 0))
    return pl.pallas_call(
        functools.partial(_moe_combine_kernel, tc=tc, final=final),
        grid_spec=pltpu.PrefetchScalarGridSpec(
            num_scalar_prefetch=2,
            grid=(m // tc,),
            in_specs=[rows(d), rows(LANES), pl.BlockSpec((1, d), lambda i, *_: (0, 0)),
                      pl.BlockSpec(memory_space=pl.ANY)],
            out_specs=rows(d),
            scratch_shapes=[pltpu.VMEM((tc * SUBLANES, LANES), F32), pltpu.VMEM((tc * SUBLANES, LANES), F32),
                            pltpu.SemaphoreType.DMA(())]),
        out_shape=jax.ShapeDtypeStruct((m, d), F32),
        compiler_params=_params("arbitrary"),
        name="moe_combine",
    )(slot1, slot2, x, top, gfin, ys)


def _moe_routing(tops, tm):
    sizes = [t.shape[0] for t in tops]
    experts = jnp.concatenate([t[:, k] for k in range(2) for t in tops]).astype(jnp.int32)
    onehot = (experts[:, None] == jnp.arange(N_EXPERTS, dtype=jnp.int32)[None, :]).astype(jnp.int32)
    rank = jnp.sum((jnp.cumsum(onehot, axis=0) - onehot) * onehot, axis=1)
    counts = jnp.sum(onehot, axis=0)
    padded = ((counts + tm - 1) // tm) * tm
    ends = jnp.cumsum(padded)
    slot = ((ends - padded)[experts] + rank).astype(jnp.int32)
    n_tiles = -(-2 * sum(sizes) // tm) + N_EXPERTS
    tile_start = jnp.arange(n_tiles, dtype=jnp.int32) * tm
    tile_expert = jnp.minimum(jnp.sum((tile_start[:, None] >= ends[None, :]).astype(jnp.int32), axis=1),
                              N_EXPERTS - 1).astype(jnp.int32)
    used_end = ends - padded + counts
    tile_rows = jnp.clip(used_end[tile_expert] - tile_start, 0, tm).astype(jnp.int32)
    offsets = [0]
    for s in sizes + sizes:
        offsets.append(offsets[-1] + s)
    pieces = [slot[offsets[j]:offsets[j + 1]] for j in range(2 * len(sizes))]
    slots = [(pieces[g], pieces[len(sizes) + g]) for g in range(len(sizes))]
    return slots, tile_expert, tile_rows, n_tiles * tm


def _final_norm_kernel(x_ref, g_ref, o_ref):
    o_ref[...] = _rms(x_ref[...], g_ref[...])


def _final_norm(x, g, tm):
    m, d = x.shape
    row = pl.BlockSpec((tm, d), lambda i: (i, 0))
    return pl.pallas_call(
        _final_norm_kernel, grid=(m // tm,), in_specs=[row, _full(g.shape)], out_specs=row,
        out_shape=jax.ShapeDtypeStruct((m, d), F32), compiler_params=_params("parallel"),
        name="final_norm",
    )(x, g)


def _row_tile(m, want):
    t = min(m, want)
    assert m % t == 0, (m, t)
    return t


def kernel(x_prompt, x_sample, cache_k, cache_v, cache_logf, cache_conv, cache_mem_k, cache_mem_v, page_table, mem_prompt, g_mix, w_in, b_fgate, w_dw, b_dw, g_cln, b_cln, g_attn_out, g_conv_out, w_out, g_cross, g_mem, w_cq, w_ckv, w_co, g_ffn, w_ff_gu, w_ff_down, w_router, b_router, w_e_gu, w_e_down, g_final):
    nb, seq, d = x_prompt.shape
    nd, t_new, _ = x_sample.shape
    depth = w_in.shape[0]
    a = ATTN_WIDTH
    cw = d - a
    n_mem = mem_prompt.shape[1]
    n_phys, page = cache_k.shape[1], cache_k.shape[2]
    hist = CONV_K - 1
    vec = lambda v: v.reshape(1, -1).astype(F32)
    pad_lanes = lambda v, fill=0.0: jnp.pad(v, [(0, 0)] * (v.ndim - 1) + [(0, LANES - v.shape[-1])],
                                            constant_values=fill)

    xs = [x_prompt.reshape(nb * seq, d), x_sample.reshape(nd * t_new, d)]
    mem2 = mem_prompt.reshape(nb * n_mem, d)
    cache_kt = cache_k.transpose(0, 1, 3, 4, 2).reshape(depth * n_phys, a, page)
    cache_vt = cache_v.transpose(0, 1, 3, 4, 2).reshape(depth * n_phys, a, page)
    cache_lft = cache_logf.astype(F32).transpose(0, 1, 3, 2).reshape(depth * n_phys, N_HEADS, page)
    mem_k_rows = cache_mem_k.astype(F32).reshape(depth * nd, n_mem * X_HEADS, X_HEAD_DIM)
    mem_v_rows = cache_mem_v.astype(F32).reshape(depth * nd, n_mem * X_HEADS, X_HEAD_DIM)
    outs = {k: [] for k in ("lfp", "cvp", "mkp", "mvp", "ks", "vs", "lfs", "cvs")}
    kv_stack = (None, None)

    for l in range(depth):
        wqkv = w_in[l][:, :3 * a].astype(BF16)
        wf = pad_lanes(w_in[l][:, 3 * a:3 * a + N_HEADS]).astype(BF16)
        wglu = w_in[l][:, 3 * a + N_HEADS:].astype(BF16)
        bf = pad_lanes(vec(b_fgate[l]))
        wa = w_out[l][:a].astype(BF16)
        wc = w_out[l][a:].astype(BF16)
        wq = w_cq[l].astype(BF16)
        wo = w_co[l].astype(BF16)
        wk_mem = w_ckv[l][:, :X_WIDTH].astype(BF16)
        wv_mem = w_ckv[l][:, X_WIDTH:].astype(BF16)
        conv_args = (w_dw[l].astype(F32), vec(b_dw[l]), vec(g_cln[l]), vec(b_cln[l]), vec(g_conv_out[l]))
        last = l == depth - 1

        mk_p, mv_p = _memory_kv(mem2, vec(g_mem[l]), wk_mem, wv_mem, _row_tile(nb * n_mem, 512))
        outs["mkp"].append(mk_p.reshape(nb, n_mem, X_HEADS, X_HEAD_DIM))
        outs["mvp"].append(mv_p.reshape(nb, n_mem, X_HEADS, X_HEAD_DIM))
        mem_rows = n_mem * X_HEADS
        mems = [(mk_p.reshape(nb, mem_rows, X_HEAD_DIM), mv_p.reshape(nb, mem_rows, X_HEAD_DIM), 0),
                (mem_k_rows, mem_v_rows, l * nd)]

        tiles = [_row_tile(x.shape[0], 512) for x in xs]
        g_f = vec(g_ffn[l])
        router = None
        if l % 2 == 1:
            w_r = pad_lanes(w_router[l // 2].astype(F32))
            w_r_hi = w_r.astype(BF16)
            router = (g_f, w_r_hi, (w_r - w_r_hi.astype(F32)).astype(BF16), pad_lanes(vec(b_router[l // 2])))
        tops = [None, None]
        for grp in range(2):
            x = xs[grp]
            m = x.shape[0]
            tm = tiles[grp]
            stacked = (seq, kv_stack[0], kv_stack[1]) if grp == 0 else None
            q, k, v, kb, vb, lf_pad, u = _proj_in(x, vec(g_mix[l]), wqkv, wf, wglu, bf, tm, stacked)
            lf = lf_pad[:, :N_HEADS]
            if grp == 0:
                c = _cumsum_logf(lf_pad, nb, seq)
                o_attn = _fox_prompt(q, kb, vb, c, nb, seq, _row_tile(seq, 512), _row_tile(seq, 512))
                u3 = u.reshape(nb, seq, cw)
                o_conv = _conv_prompt(u3, *conv_args, _row_tile(seq, 512)).reshape(m, cw)
                kv_stack = (k, v)
                outs["lfp"].append(lf.reshape(nb, seq, N_HEADS))
                outs["cvp"].append(u3[:, seq - hist:])
                n_grp, t_grp = nb, seq
            else:
                new_t = lambda z: jnp.pad(z.reshape(nd, t_new, -1).transpose(0, 2, 1),
                                          ((0, 0), (0, 0), (0, page - t_new)))
                o_attn = _fox_sample(page_table, l * n_phys, q, new_t(k).reshape(nd * a, page),
                                     new_t(v).reshape(nd * a, page), new_t(lf),
                                     cache_kt, cache_vt, cache_lft, t_new)
                u_hist = jnp.concatenate([cache_conv[l].astype(F32), u.reshape(nd, t_new, cw)], axis=1)
                o_conv = _conv_sample(u_hist, *conv_args, t_new).reshape(m, cw)
                outs["ks"].append(k.reshape(nd, t_new, N_HEADS, HEAD_DIM))
                outs["vs"].append(v.reshape(nd, t_new, N_HEADS, HEAD_DIM))
                outs["lfs"].append(lf.reshape(nd, t_new, N_HEADS))
                outs["cvs"].append(u_hist[:, -hist:])
                n_grp, t_grp = nd, t_new
            xs[grp], tops[grp] = _mix_cross(x, o_attn, o_conv, vec(g_attn_out[l]), wa, wc, vec(g_cross[l]), wq,
                                            mems[grp][0], mems[grp][1], wo, n_grp, mems[grp][2],
                                            _row_tile(t_grp, 512), router)

        if l % 2 == 0:
            wgu_d, wd_d = w_ff_gu[l // 2].astype(BF16), w_ff_down[l // 2].astype(BF16)
            xs = [_ffn(x, g_f, wgu_d, wd_d, tm, 256) for x, tm in zip(xs, tiles)]
            if last:
                xs = [_final_norm(x, vec(g_final), tm) for x, tm in zip(xs, tiles)]
        else:
            e = l // 2
            slots, tile_expert, tile_rows, n_slots = _moe_routing(tops, MOE_TILE)
            slot_rows = jnp.zeros((n_slots * SUBLANES, LANES), F32)
            for x, (s1, s2) in zip(xs, slots):
                slot_rows = _moe_dispatch(s1, s2, x, g_f, slot_rows, _row_tile(x.shape[0], 256))
            ys = _moe_sparse(tile_expert, tile_rows, slot_rows, w_e_gu[e], w_e_down[e], MOE_TILE, 512)
            xs = [_moe_combine(s1, s2, x, top, vec(g_final), ys, last, _row_tile(x.shape[0], 256))
                  for x, top, (s1, s2) in zip(xs, tops, slots)]

    st = lambda key: jnp.stack(outs[key])
    kv_prompt = lambda z: z.reshape(depth, nb, N_HEADS, HEAD_DIM, seq).transpose(0, 1, 4, 2, 3)
    return (xs[0].reshape(nb, seq, d), xs[1].reshape(nd, t_new, d),
            kv_prompt(kv_stack[0]), kv_prompt(kv_stack[1]), st("lfp"), st("cvp"), st("mkp"), st("mvp"),
            st("ks"), st("vs"), st("lfs"), st("cvs"))
```

```python
import functools

import jax
import jax.numpy as jnp
from jax import lax
from jax.experimental import pallas as pl
from jax.experimental.pallas import tpu as pltpu

EPS = 1e-6
N_HEADS = 8
HEAD_DIM = 64
ATTN_WIDTH = N_HEADS * HEAD_DIM
CONV_K = 31
CONV_HALO = 32
X_HEADS = 4
X_HEAD_DIM = 128
X_WIDTH = X_HEADS * X_HEAD_DIM
N_EXPERTS = 8
LANES = 128
SUBLANES = 8
NEG = -1e30
LOG2E = 1.4426950408889634
VMEM_LIMIT_BYTES = 56 * 1024 * 1024
PAGES_PER_STEP = 32
MOE_TILE = 1024

F32 = jnp.float32
BF16 = jnp.bfloat16
HIGHEST = lax.Precision.HIGHEST


def _params(*sem):
    return pltpu.CompilerParams(dimension_semantics=sem, vmem_limit_bytes=VMEM_LIMIT_BYTES)


def _rms(x, g):
    return x * lax.rsqrt(jnp.mean(x * x, axis=-1, keepdims=True) + EPS) * g


def _sigmoid(x):
    return 1.0 / (1.0 + jnp.exp(-x))


def _dot(a, b):
    return jnp.dot(a, b, preferred_element_type=F32)


def _dot_nt(a, b):
    return lax.dot_general(a, b, (((1,), (1,)), ((), ())), preferred_element_type=F32)


def _full(shape):
    nd = len(shape)
    return pl.BlockSpec(shape, lambda *_: (0,) * nd)


def _proj_in_kernel(x_ref, g_ref, wqkv_ref, wf_ref, wglu_ref, bf_ref,
                    q_ref, k_ref, v_ref, kb_ref, vb_ref, lf_ref, u_ref, *, transposed):
    a = ATTN_WIDTH
    h = _rms(x_ref[...], g_ref[...]).astype(BF16)
    qkv = _dot(h, wqkv_ref[...])
    q_ref[...] = (qkv[:, :a] * (HEAD_DIM ** -0.5 * LOG2E)).astype(BF16)
    k = qkv[:, a:2 * a]
    v = qkv[:, 2 * a:]
    if transposed:
        k_ref[0, 0] = k.T
        v_ref[0, 0] = v.T
    else:
        k_ref[...] = k
        v_ref[...] = v
    kb_ref[...] = k.astype(BF16)
    vb_ref[...] = v.astype(BF16)
    z = _dot(h, wf_ref[...]) + bf_ref[...]
    lf_ref[...] = jnp.minimum(z, 0.0) - jnp.log1p(jnp.exp(-jnp.abs(z)))
    glu = _dot(h, wglu_ref[...])
    c = glu.shape[1] // 2
    u_ref[...] = glu[:, :c] * _sigmoid(glu[:, c:])


def _proj_in_carry_kernel(inner, x_ref, g_ref, wqkv_ref, wf_ref, wglu_ref, bf_ref, k_prev, v_prev,
                          q_ref, k_ref, v_ref, *out_refs):
    n_prev = k_prev.shape[0]
    k_ref[0:n_prev] = k_prev[...]
    v_ref[0:n_prev] = v_prev[...]
    inner(x_ref, g_ref, wqkv_ref, wf_ref, wglu_ref, bf_ref, q_ref, k_ref.at[n_prev:], v_ref.at[n_prev:], *out_refs)


def _proj_in(x, g, wqkv, wf, wglu, bf, tm, stacked=None):
    m, d = x.shape
    a = ATTN_WIDTH
    c = wglu.shape[1] // 2
    row = lambda w: pl.BlockSpec((tm, w), lambda i: (i, 0))
    args = [x, g, wqkv, wf, wglu, bf]
    in_specs = [row(d), _full((1, d)), _full(wqkv.shape), _full(wf.shape), _full(wglu.shape), _full((1, LANES))]
    if stacked is None:
        kv_spec, kv_shape = row(a), jax.ShapeDtypeStruct((m, a), F32)
        kernel_fn = functools.partial(_proj_in_kernel, transposed=False)
    else:
        seq_len, k_prev, v_prev = stacked
        per = seq_len // tm
        n_prev = 0 if k_prev is None else k_prev.shape[0]
        slabs = lambda n: pl.BlockSpec((n, 1, a, tm), lambda i: (0, i // per, 0, i % per))
        kv_spec = slabs(n_prev + 1)
        kv_shape = jax.ShapeDtypeStruct((n_prev + 1, m // seq_len, a, seq_len), F32)
        kernel_fn = functools.partial(_proj_in_kernel, transposed=True)
        if n_prev:
            args += [k_prev, v_prev]
            in_specs += [slabs(n_prev)] * 2
            kernel_fn = functools.partial(_proj_in_carry_kernel, kernel_fn)
    return pl.pallas_call(
        kernel_fn,
        grid=(m // tm,),
        in_specs=in_specs,
        out_specs=[row(a), kv_spec, kv_spec, row(a), row(a), row(LANES), row(c)],
        out_shape=[jax.ShapeDtypeStruct((m, a), BF16), kv_shape, kv_shape, jax.ShapeDtypeStruct((m, a), BF16),
                   jax.ShapeDtypeStruct((m, a), BF16), jax.ShapeDtypeStruct((m, LANES), F32),
                   jax.ShapeDtypeStruct((m, c), F32)],
        compiler_params=_params("parallel"),
        name="proj_in",
    )(*args)


def _cumsum_kernel(lf_ref, c_ref):
    s = lf_ref.shape[0]
    lft = lf_ref[...].T[:N_HEADS, :]
    r = lax.broadcasted_iota(jnp.int32, (LANES, LANES), 0)
    cidx = lax.broadcasted_iota(jnp.int32, (LANES, LANES), 1)
    tri = (r <= cidx).astype(F32)
    carry = jnp.zeros((N_HEADS, 1), F32)
    for b in range(s // LANES):
        cb = jnp.dot(lft[:, b * LANES:(b + 1) * LANES], tri, precision=HIGHEST,
                     preferred_element_type=F32) + carry
        c_ref[0, :, b * LANES:(b + 1) * LANES] = cb * LOG2E
        carry = cb[:, LANES - 1:LANES]


def _cumsum_logf(lf_pad, nb, s):
    return pl.pallas_call(
        _cumsum_kernel,
        grid=(nb,),
        in_specs=[pl.BlockSpec((s, LANES), lambda b: (b, 0))],
        out_specs=pl.BlockSpec((1, N_HEADS, s), lambda b: (b, 0, 0)),
        out_shape=jax.ShapeDtypeStruct((nb, N_HEADS, s), F32),
        compiler_params=_params("parallel"),
        name="cumsum_logf",
    )(lf_pad)


def _fox_prompt_kernel(q_ref, k_ref, v_ref, c_ref, o_ref, m_scr, l_scr, acc_scr, *, tq, tk):
    i = pl.program_id(2)
    lane = lax.broadcasted_iota(jnp.int32, (1, LANES), 1)
    q = q_ref[...]
    zero = jnp.zeros_like(q)
    qh = (jnp.where(lane < HEAD_DIM, q, zero), jnp.where(lane >= HEAD_DIM, q, zero))
    m_scr[...] = jnp.full(m_scr.shape, NEG, F32)
    l_scr[...] = jnp.zeros(l_scr.shape, F32)
    acc_scr[...] = jnp.zeros(acc_scr.shape, F32)
    row = lax.broadcasted_iota(jnp.int32, (tq, tk), 0)
    col = lax.broadcasted_iota(jnp.int32, (tq, tk), 1)
    ratio = tq // tk

    def step(j, diag):
        start = pl.multiple_of(j * tk, tk)
        kb = k_ref[pl.ds(start, tk), :]
        vb = v_ref[pl.ds(start, tk), :]
        cs = c_ref[:, j]
        for hh in range(2):
            s = _dot_nt(qh[hh], kb) - cs[hh]
            if diag is not None:
                s = jnp.where(col + diag * tk <= row, s, NEG)
            m_prev = m_scr[hh]
            m_new = jnp.maximum(m_prev, jnp.max(s, axis=-1, keepdims=True))
            alpha = jnp.exp2(m_prev - m_new)
            p = jnp.exp2(s - jnp.concatenate([m_new] * (tk // LANES), axis=1))
            l_scr[hh] = alpha * l_scr[hh] + jnp.sum(p, axis=-1, keepdims=True)
            acc_scr[hh] = alpha * acc_scr[hh] + _dot(p.astype(BF16), vb)
            m_scr[hh] = m_new

    def body(j, carry):
        step(j, None)
        return carry

    lax.fori_loop(0, i * ratio, body, 0)
    for dg in range(ratio):
        step(i * ratio + dg, dg)
    o_ref[...] = jnp.where(lane < HEAD_DIM, acc_scr[0] / l_scr[0], acc_scr[1] / l_scr[1])


def _fox_prompt(q, kb, vb, c, nb, s, tq, tk):
    m = q.shape[0]
    nt = s // tq
    nk = s // tk
    c4 = c.reshape(nb * N_HEADS, nk, 1, tk)
    pairs = N_HEADS // 2
    return pl.pallas_call(
        functools.partial(_fox_prompt_kernel, tq=tq, tk=tk),
        grid=(nb, pairs, nt),
        in_specs=[pl.BlockSpec((tq, LANES), lambda b, p, i: (b * nt + i, p)),
                  pl.BlockSpec((s, LANES), lambda b, p, i: (b, p)),
                  pl.BlockSpec((s, LANES), lambda b, p, i: (b, p)),
                  pl.BlockSpec((2, nk, 1, tk), lambda b, p, i: (b * pairs + p, 0, 0, 0))],
        out_specs=pl.BlockSpec((tq, LANES), lambda b, p, i: (b * nt + i, p)),
        out_shape=jax.ShapeDtypeStruct((m, ATTN_WIDTH), F32),
        scratch_shapes=[pltpu.VMEM((2, tq, LANES), F32), pltpu.VMEM((2, tq, LANES), F32),
                        pltpu.VMEM((2, tq, LANES), F32)],
        compiler_params=_params("parallel", "parallel", "arbitrary"),
        name="fox_prompt",
    )(q, kb, vb, c4)


def _fox_sample_kernel(pt_ref, q_ref, kn_ref, vn_ref, lfn_ref, *rest, t_new, n_chunks):
    pg = PAGES_PER_STEP
    k_refs, v_refs, lf_refs = rest[:pg], rest[pg:2 * pg], rest[2 * pg:3 * pg]
    o_ref, m_scr, l_scr, acc_scr, carry_scr = rest[3 * pg:]
    del pt_ref
    j = pl.program_id(1)
    rows = N_HEADS * t_new
    width = ATTN_WIDTH
    page = lf_refs[0].shape[2]

    @pl.when(j == 0)
    def _():
        m_scr[...] = jnp.full(m_scr.shape, NEG, F32)
        l_scr[...] = jnp.zeros(l_scr.shape, F32)
        acc_scr[...] = jnp.zeros(acc_scr.shape, F32)
        carry_scr[...] = jnp.zeros(carry_scr.shape, F32)

    q = q_ref[...]
    q_rows = jnp.concatenate([q] * N_HEADS, axis=0)
    rr = lax.broadcasted_iota(jnp.int32, (rows, width), 0)
    ll = lax.broadcasted_iota(jnp.int32, (rows, width), 1)
    head_mask = (rr // t_new) == (ll // HEAD_DIM)
    q_rows = jnp.where(head_mask, q_rows, jnp.zeros_like(q_rows))

    def per_query_rows(x):
        return jnp.concatenate([jnp.broadcast_to(x[h:h + 1], (t_new, x.shape[1])) for h in range(N_HEADS)],
                               axis=0)

    def attend(kts, vts, biases):
        s = jnp.concatenate([_dot(q_rows, kt) + b for kt, b in zip(kts, biases)], axis=1)
        m_prev = m_scr[...]
        m_new = jnp.maximum(m_prev, jnp.max(s, axis=-1, keepdims=True))
        alpha = jnp.exp2(m_prev - m_new)
        p = jnp.exp2(s - m_new)
        l_scr[...] = alpha * l_scr[...] + jnp.sum(p, axis=-1, keepdims=True)
        p = p.astype(BF16)
        pv = _dot_nt(p[:, :page], vts[0])
        for g in range(1, len(vts)):
            pv = pv + _dot_nt(p[:, g * page:(g + 1) * page], vts[g])
        acc_scr[...] = alpha * acc_scr[...] + pv
        m_scr[...] = m_new

    lf = jnp.concatenate([lf_refs[g][0] for g in range(pg)], axis=0)
    lane_all = lax.broadcasted_iota(jnp.int32, lf.shape, 1)
    suffix = lf
    shift = 1
    while shift < page:
        nxt = pltpu.roll(suffix, page - shift, axis=1)
        suffix = suffix + jnp.where(lane_all + shift < page, nxt, 0.0)
        shift *= 2
    within = suffix - lf
    totals = jnp.sum(lf, axis=-1, keepdims=True)
    later = carry_scr[...]
    biases = [None] * pg
    for g in reversed(range(pg)):
        biases[g] = per_query_rows((within[g * N_HEADS:(g + 1) * N_HEADS] + later) * LOG2E)
        later = later + totals[g * N_HEADS:(g + 1) * N_HEADS]
    carry_scr[...] = later
    attend([k_refs[g][0].astype(BF16) for g in range(pg)], [v_refs[g][0].astype(BF16) for g in range(pg)], biases)
    lane = lax.broadcasted_iota(jnp.int32, (N_HEADS, page), 1)

    @pl.when(j == n_chunks - 1)
    def _():
        c_new = lfn_ref[0]
        shift = 1
        while shift < t_new:
            c_new = c_new + jnp.where(lane >= shift, pltpu.roll(c_new, shift, axis=1), 0.0)
            shift *= 2
        br = lax.broadcasted_iota(jnp.int32, (rows, page), 0)
        bc = lax.broadcasted_iota(jnp.int32, (rows, page), 1)
        bias = jnp.where(bc <= (br % t_new), per_query_rows(c_new * (-LOG2E)), NEG)
        attend([kn_ref[...].astype(BF16)], [vn_ref[...].astype(BF16)], [bias])
        o = jnp.where(head_mask, acc_scr[...] / l_scr[...], 0.0)
        out = o[0:t_new]
        for h in range(1, N_HEADS):
            out = out + o[h * t_new:(h + 1) * t_new]
        o_ref[...] = out


def _fox_sample(page_table, base, q, kt_new, vt_new, lft_new, cache_kt, cache_vt, cache_lft, t_new):
    n, n_pages = page_table.shape
    pg = PAGES_PER_STEP
    n_chunks = n_pages // pg
    assert n_pages % pg == 0
    width, page = cache_kt.shape[1], cache_kt.shape[2]
    rows = N_HEADS * t_new

    def page_spec(shape, g):
        return pl.BlockSpec(shape, lambda b, j, pt: (base + pt[b, (n_chunks - 1 - j) * pg + g], 0, 0))

    in_specs = ([pl.BlockSpec((t_new, width), lambda b, j, pt: (b, 0)),
                 pl.BlockSpec((width, page), lambda b, j, pt: (b, 0)),
                 pl.BlockSpec((width, page), lambda b, j, pt: (b, 0)),
                 pl.BlockSpec((1, N_HEADS, page), lambda b, j, pt: (b, 0, 0))]
                + [page_spec((1, width, page), g) for g in range(pg)]
                + [page_spec((1, width, page), g) for g in range(pg)]
                + [page_spec((1, N_HEADS, page), g) for g in range(pg)])
    return pl.pallas_call(
        functools.partial(_fox_sample_kernel, t_new=t_new, n_chunks=n_chunks),
        grid_spec=pltpu.PrefetchScalarGridSpec(
            num_scalar_prefetch=1,
            grid=(n, n_chunks),
            in_specs=in_specs,
            out_specs=pl.BlockSpec((t_new, width), lambda b, j, pt: (b, 0)),
            scratch_shapes=[pltpu.VMEM((rows, 1), F32), pltpu.VMEM((rows, 1), F32),
                            pltpu.VMEM((rows, width), F32), pltpu.VMEM((N_HEADS, 1), F32)]),
        out_shape=jax.ShapeDtypeStruct((n * t_new, width), F32),
        compiler_params=_params("parallel", "arbitrary"),
        name="fox_sample",
    )(page_table, q, kt_new, vt_new, lft_new, *([cache_kt] * pg), *([cache_vt] * pg), *([cache_lft] * pg))


def _conv_tail(z, b_ref, gl_ref, bl_ref, gco_ref):
    z = z + b_ref[...]
    zc = z - jnp.mean(z, axis=-1, keepdims=True)
    y = zc * lax.rsqrt(jnp.mean(zc * zc, axis=-1, keepdims=True) + EPS) * gl_ref[...] + bl_ref[...]
    y = y * _sigmoid(y)
    return _rms(y, gco_ref[...]).astype(BF16)


def _conv_prompt_kernel(u_ref, halo_ref, w_ref, b_ref, gl_ref, bl_ref, gco_ref, o_ref, xs_ref, sh_ref, *, ts, rc):
    i = pl.program_id(1)
    halo = halo_ref[0]
    xs_ref[0:CONV_HALO, :] = jnp.where(i > 0, halo, jnp.zeros_like(halo))
    xs_ref[CONV_HALO:CONV_HALO + ts, :] = u_ref[0]
    n_sh = sh_ref.shape[1]
    for r in range(1, SUBLANES):
        sh_ref[r - 1] = xs_ref[r:r + n_sh, :]
    off = CONV_HALO - (CONV_K - 1)
    width = u_ref.shape[2]
    for c in range(ts // rc):
        acc = jnp.zeros((rc // SUBLANES, SUBLANES, width), F32)
        for j in range(CONV_K):
            q, r = divmod(off + j, SUBLANES)
            lo = c * rc + q * SUBLANES
            rows = xs_ref[lo:lo + rc, :] if r == 0 else sh_ref[r - 1, lo:lo + rc, :]
            acc = acc + w_ref[j][None] * rows.reshape(rc // SUBLANES, SUBLANES, width)
        o_ref[0, c * rc:(c + 1) * rc, :] = _conv_tail(acc.reshape(rc, width), b_ref, gl_ref, bl_ref, gco_ref)


def _conv_prompt(u3, w, b, gl, bl, gco, ts):
    nb, s, c = u3.shape
    hb = ts // CONV_HALO
    vec = _full((1, c))
    w = jnp.broadcast_to(w[:, None, :], (w.shape[0], SUBLANES, c))
    return pl.pallas_call(
        functools.partial(_conv_prompt_kernel, ts=ts, rc=32),
        grid=(nb, s // ts),
        in_specs=[pl.BlockSpec((1, ts, c), lambda n, i: (n, i, 0)),
                  pl.BlockSpec((1, CONV_HALO, c), lambda n, i: (n, jnp.maximum(i * hb - 1, 0), 0)),
                  _full(w.shape), vec, vec, vec, vec],
        out_specs=pl.BlockSpec((1, ts, c), lambda n, i: (n, i, 0)),
        out_shape=jax.ShapeDtypeStruct((nb, s, c), BF16),
        scratch_shapes=[pltpu.VMEM((CONV_HALO + ts, c), F32),
                        pltpu.VMEM((SUBLANES - 1, CONV_HALO + ts - SUBLANES, c), F32)],
        compiler_params=_params("parallel", "parallel"),
        name="conv_prompt",
    )(u3, u3, w, b, gl, bl, gco)


def _conv_sample_kernel(uh_ref, w_ref, b_ref, gl_ref, bl_ref, gco_ref, o_ref, *, t_new):
    def one_sequence(n, carry):
        acc = jnp.zeros((t_new, uh_ref.shape[2]), F32)
        for j in range(CONV_K):
            acc = acc + w_ref[j:j + 1, :] * uh_ref[n, j:j + t_new, :]
        o_ref[n] = _conv_tail(acc, b_ref, gl_ref, bl_ref, gco_ref)
        return carry

    lax.fori_loop(0, uh_ref.shape[0], one_sequence, 0)


def _conv_sample(u_hist, w, b, gl, bl, gco, t_new):
    n, hl, c = u_hist.shape
    vec = _full((1, c))
    return pl.pallas_call(
        functools.partial(_conv_sample_kernel, t_new=t_new),
        grid=(1,),
        in_specs=[_full(u_hist.shape), _full(w.shape), vec, vec, vec, vec],
        out_specs=_full((n, t_new, c)),
        out_shape=jax.ShapeDtypeStruct((n, t_new, c), BF16),
        compiler_params=_params("arbitrary"),
        name="conv_sample",
    )(u_hist, w, b, gl, bl, gco)


def _memory_kv_kernel(mem_ref, g_ref, wk_ref, wv_ref, mk_ref, mv_ref):
    h = _rms(mem_ref[...], g_ref[...]).astype(BF16)
    rows = mem_ref.shape[0]
    for ref, w_ref in ((mk_ref, wk_ref), (mv_ref, wv_ref)):
        kv = _dot(h, w_ref[...])
        for hh in range(X_HEADS):
            ref[pl.ds(hh, rows, stride=X_HEADS), :] = kv[:, hh * X_HEAD_DIM:(hh + 1) * X_HEAD_DIM]


def _memory_kv(mem, g, wk, wv, tm):
    m, d = mem.shape
    out = pl.BlockSpec((tm * X_HEADS, X_HEAD_DIM), lambda i: (i, 0))
    return pl.pallas_call(
        _memory_kv_kernel,
        grid=(m // tm,),
        in_specs=[pl.BlockSpec((tm, d), lambda i: (i, 0)), _full(g.shape), _full(wk.shape), _full(wv.shape)],
        out_specs=[out, out],
        out_shape=[jax.ShapeDtypeStruct((m * X_HEADS, X_HEAD_DIM), F32)] * 2,
        compiler_params=_params("parallel"),
        name="memory_kv",
    )(mem, g, wk, wv)


def _mix_cross_kernel(x_ref, oa_ref, oc_ref, ga_ref, wa_ref, wc_ref, g_ref, wq_ref, mk_ref, mv_ref, wo_ref,
                      *rest, routed):
    o_ref = rest[4] if routed else rest[0]
    a = _rms(oa_ref[...], ga_ref[...]).astype(BF16)
    x = x_ref[...] + _dot(a, wa_ref[...]) + _dot(oc_ref[...], wc_ref[...])
    h = _rms(x, g_ref[...]).astype(BF16)
    q = (_dot(h, wq_ref[...]) * (X_HEAD_DIM ** -0.5)).astype(BF16)
    n_mem = mk_ref.shape[1] // X_HEADS
    outs = []
    for hh in range(X_HEADS):
        head = pl.ds(hh, n_mem, stride=X_HEADS)
        s = _dot_nt(q[:, hh * X_HEAD_DIM:(hh + 1) * X_HEAD_DIM], mk_ref[0, head, :].astype(BF16))
        p = jnp.exp(s - jnp.max(s, axis=-1, keepdims=True))
        o = _dot(p.astype(BF16), mv_ref[0, head, :].astype(BF16)) / jnp.sum(p, axis=-1, keepdims=True)
        outs.append(o.astype(BF16))
    x = x + _dot(jnp.concatenate(outs, axis=1), wo_ref[...])
    o_ref[...] = x
    if routed:
        gf_ref, whi_ref, wlo_ref, br_ref, _, top_ref = rest
        top_ref[...] = _route(x, gf_ref[...], (whi_ref[...], wlo_ref[...]), br_ref[...])


def _mix_cross(x, o_attn, o_conv, ga, wa, wc, g, wq, mk, mv, wo, nb, base, tm, router=None):
    m, d = x.shape
    per = (m // nb) // tm
    mem_rows = mk.shape[1]
    rows = lambda w: pl.BlockSpec((tm, w), lambda b, i: (b * per + i, 0))
    mem = pl.BlockSpec((1, mem_rows, X_HEAD_DIM), lambda b, i: (base + b, 0, 0))
    args = [x, o_attn, o_conv, ga, wa, wc, g, wq, mk, mv, wo]
    in_specs = [rows(d), rows(o_attn.shape[1]), rows(o_conv.shape[1]), _full(ga.shape), _full(wa.shape),
                _full(wc.shape), _full(g.shape), _full(wq.shape), mem, mem, _full(wo.shape)]
    out_specs, out_shape = [rows(d)], [jax.ShapeDtypeStruct((m, d), F32)]
    if router is not None:
        args += list(router)
        in_specs += [_full(r.shape) for r in router]
        out_specs.append(rows(LANES))
        out_shape.append(jax.ShapeDtypeStruct((m, LANES), F32))
    res = pl.pallas_call(
        functools.partial(_mix_cross_kernel, routed=router is not None),
        grid=(nb, per),
        in_specs=in_specs,
        out_specs=out_specs,
        out_shape=out_shape,
        compiler_params=_params("parallel", "parallel"),
        name="mix_cross",
    )(*args)
    return (res[0], res[1]) if router is not None else (res[0], None)


def _ffn_kernel(x_ref, g_ref, wgu_ref, wd_ref, o_ref, acc_ref, *, tf):
    x = x_ref[...]
    h = _rms(x, g_ref[...]).astype(BF16)
    f = wd_ref.shape[0]
    acc_ref[...] = x
    for c in range(f // tf):
        gt = _dot(h, wgu_ref[:, c * tf:(c + 1) * tf])
        up = _dot(h, wgu_ref[:, f + c * tf:f + (c + 1) * tf])
        act = (gt * _sigmoid(gt) * up).astype(BF16)
        acc_ref[...] += _dot(act, wd_ref[c * tf:(c + 1) * tf, :])
    o_ref[...] = acc_ref[...]


def _ffn(x, g, wgu, wd, tm, tf):
    m, d = x.shape
    row = pl.BlockSpec((tm, d), lambda i: (i, 0))
    resident = lambda shape: pl.BlockSpec(shape, lambda i: (0, 0), pipeline_mode=pl.Buffered(1))
    return pl.pallas_call(
        functools.partial(_ffn_kernel, tf=tf),
        grid=(m // tm,),
        in_specs=[row, _full(g.shape), resident(wgu.shape), resident(wd.shape)],
        out_specs=row,
        out_shape=jax.ShapeDtypeStruct((m, d), F32),
        scratch_shapes=[pltpu.VMEM((tm, d), F32)],
        compiler_params=_params("parallel"),
        name="ffn_dense",
    )(x, g, wgu, wd)


def _route(x, g, wr, br):
    h = _rms(x, g)
    h_hi = h.astype(BF16)
    h_lo = (h - h_hi.astype(F32)).astype(BF16)
    w_hi, w_lo = wr
    logits = _dot(h_hi, w_hi) + _dot(h_hi, w_lo) + _dot(h_lo, w_hi) + br
    lane = lax.broadcasted_iota(jnp.int32, logits.shape, 1)
    logits = jnp.where(lane < N_EXPERTS, logits, NEG)
    p = jnp.exp(logits - jnp.max(logits, axis=-1, keepdims=True))
    p = p / jnp.sum(p, axis=-1, keepdims=True)
    v1 = jnp.max(p, axis=-1, keepdims=True)
    i1 = jnp.min(jnp.where(p == v1, lane, LANES), axis=-1, keepdims=True)
    rest = jnp.where(lane == i1, -1.0, p)
    v2 = jnp.max(rest, axis=-1, keepdims=True)
    i2 = jnp.min(jnp.where(rest == v2, lane, LANES), axis=-1, keepdims=True)
    tot = v1 + v2
    g1 = v1 / tot
    g2 = v2 / tot
    return jnp.where(lane == 0, i1.astype(F32),
                     jnp.where(lane == 1, i2.astype(F32), jnp.where(lane == 2, g1, jnp.where(lane == 3, g2, 0.0))))


def _to_tiles(ref, x):
    for a in range(SUBLANES):
        ref[pl.ds(a, x.shape[0], stride=SUBLANES), :] = x[:, a * LANES:(a + 1) * LANES]


def _from_tiles(ref):
    rows = ref.shape[0] // SUBLANES
    return jnp.concatenate([ref[pl.ds(a, rows, stride=SUBLANES), :] for a in range(SUBLANES)], axis=1)


def _tile_at(ref, slot):
    return ref.at[pl.ds(pl.multiple_of(slot * SUBLANES, SUBLANES), SUBLANES)]


def _moe_dispatch_kernel(s1_ref, s2_ref, x_ref, g_ref, xs_in, xs_hbm, stage, sem, *, td):
    del xs_in
    i = pl.program_id(0)
    _to_tiles(stage, _rms(x_ref[...], g_ref[...]))

    def issue(r, carry):
        src = _tile_at(stage, r)
        pltpu.make_async_copy(src, _tile_at(xs_hbm, s1_ref[i * td + r]), sem).start(priority=0)
        pltpu.make_async_copy(src, _tile_at(xs_hbm, s2_ref[i * td + r]), sem).start(priority=1)
        return carry

    lax.fori_loop(0, td, issue, 0, unroll=8)
    for _ in range(2):
        pltpu.make_async_copy(stage, xs_hbm.at[pl.ds(0, td * SUBLANES)], sem).wait()


def _moe_dispatch(slot1, slot2, x, g, xs, td):
    m, d = x.shape
    return pl.pallas_call(
        functools.partial(_moe_dispatch_kernel, td=td),
        grid_spec=pltpu.PrefetchScalarGridSpec(
            num_scalar_prefetch=2,
            grid=(m // td,),
            in_specs=[pl.BlockSpec((td, d), lambda i, *_: (i, 0)), pl.BlockSpec((1, d), lambda i, *_: (0, 0)),
                      pl.BlockSpec(memory_space=pl.ANY)],
            out_specs=pl.BlockSpec(memory_space=pl.ANY),
            scratch_shapes=[pltpu.VMEM((td * SUBLANES, LANES), F32), pltpu.SemaphoreType.DMA(())]),
        out_shape=jax.ShapeDtypeStruct(xs.shape, F32),
        input_output_aliases={4: 0},
        compiler_params=_params("arbitrary"),
        name="moe_dispatch",
    )(slot1, slot2, x, g, xs)


def _moe_sparse_kernel(te_ref, rows_ref, x_ref, wg_ref, wu_ref, wd_ref, o_ref, h_scr, acc_scr):
    del te_ref
    i = pl.program_id(0)
    f = pl.program_id(1)
    last = pl.num_programs(1) - 1
    rows = rows_ref[i]
    tm = h_scr.shape[0]
    half = tm // 2

    @pl.when(f == 0)
    def _():
        acc_scr[...] = jnp.zeros(acc_scr.shape, F32)
        h_scr[...] = _from_tiles(x_ref).astype(BF16)

    def swiglu(n):
        h = h_scr[0:n]
        gt = _dot(h, wg_ref[0].astype(BF16))
        up = _dot(h, wu_ref[0].astype(BF16))
        act = (gt * _sigmoid(gt) * up).astype(BF16)
        acc_scr[0:n] += _dot(act, wd_ref[0].astype(BF16))

    @pl.when(rows > half)
    def _():
        swiglu(tm)

    @pl.when((rows > 0) & (rows <= half))
    def _():
        swiglu(half)

    @pl.when(f == last)
    def _():
        _to_tiles(o_ref, acc_scr[...])


def _moe_sparse(tile_expert, tile_rows, xs, wgu, wd, tm, tf):
    n_slots = xs.shape[0] // SUBLANES
    d, f_all = wd.shape[2], wd.shape[1]
    nf = f_all // tf

    def chunk(i, f, nv):
        return jnp.where(nv[i] > 0, f, nf - 1)

    tiles = pl.BlockSpec((tm * SUBLANES, LANES), lambda i, f, te, nv: (i, 0))
    return pl.pallas_call(
        _moe_sparse_kernel,
        grid_spec=pltpu.PrefetchScalarGridSpec(
            num_scalar_prefetch=2,
            grid=(n_slots // tm, nf),
            in_specs=[tiles,
                      pl.BlockSpec((1, d, tf), lambda i, f, te, nv: (te[i], 0, chunk(i, f, nv))),
                      pl.BlockSpec((1, d, tf), lambda i, f, te, nv: (te[i], 0, nf + chunk(i, f, nv))),
                      pl.BlockSpec((1, tf, d), lambda i, f, te, nv: (te[i], chunk(i, f, nv), 0))],
            out_specs=tiles,
            scratch_shapes=[pltpu.VMEM((tm, d), BF16), pltpu.VMEM((tm, d), F32)]),
        out_shape=jax.ShapeDtypeStruct(xs.shape, F32),
        compiler_params=_params("arbitrary", "arbitrary"),
        name="moe_sparse",
    )(tile_expert, tile_rows, xs, wgu, wgu, wd)


def _moe_combine_kernel(s1_ref, s2_ref, x_ref, top_ref, gfin_ref, ys_hbm, o_ref, buf_a, buf_b, sem, *, tc, final):
    i = pl.program_id(0)

    def issue(r, carry):
        pltpu.make_async_copy(_tile_at(ys_hbm, s1_ref[i * tc + r]), _tile_at(buf_a, r), sem).start(priority=0)
        pltpu.make_async_copy(_tile_at(ys_hbm, s2_ref[i * tc + r]), _tile_at(buf_b, r), sem).start(priority=1)
        return carry

    lax.fori_loop(0, tc, issue, 0, unroll=8)
    pltpu.make_async_copy(ys_hbm.at[pl.ds(0, tc * SUBLANES)], buf_a, sem).wait()
    pltpu.make_async_copy(ys_hbm.at[pl.ds(0, tc * SUBLANES)], buf_b, sem).wait()
    top = top_ref[...]
    y = x_ref[...] + top[:, 2:3] * _from_tiles(buf_a) + top[:, 3:4] * _from_tiles(buf_b)
    o_ref[...] = _rms(y, gfin_ref[...]) if final else y


def _moe_combine(slot1, slot2, x, top, gfin, ys, final, tc):
    m, d = x.shape
    rows = lambda w: pl.BlockSpec((tc, w), lambda i, *_: (i, 0))
    return pl.pallas_call(
        functools.partial(_moe_combine_kernel, tc=tc, final=final),
        grid_spec=pltpu.PrefetchScalarGridSpec(
            num_scalar_prefetch=2,
            grid=(m // tc,),
            in_specs=[rows(d), rows(LANES), pl.BlockSpec((1, d), lambda i, *_: (0, 0)),
                      pl.BlockSpec(memory_space=pl.ANY)],
            out_specs=rows(d),
            scratch_shapes=[pltpu.VMEM((tc * SUBLANES, LANES), F32), pltpu.VMEM((tc * SUBLANES, LANES), F32),
                            pltpu.SemaphoreType.DMA(())]),
        out_shape=jax.ShapeDtypeStruct((m, d), F32),
        compiler_params=_params("arbitrary"),
        name="moe_combine",
    )(slot1, slot2, x, top, gfin, ys)


def _moe_routing(tops, tm):
    sizes = [t.shape[0] for t in tops]
    experts = jnp.concatenate([t[:, k] for k in range(2) for t in tops]).astype(jnp.int32)
    onehot = (experts[:, None] == jnp.arange(N_EXPERTS, dtype=jnp.int32)[None, :]).astype(jnp.int32)
    rank = jnp.sum((jnp.cumsum(onehot, axis=0) - onehot) * onehot, axis=1)
    counts = jnp.sum(onehot, axis=0)
    padded = ((counts + tm - 1) // tm) * tm
    ends = jnp.cumsum(padded)
    slot = ((ends - padded)[experts] + rank).astype(jnp.int32)
    n_tiles = -(-2 * sum(sizes) // tm) + N_EXPERTS
    tile_start = jnp.arange(n_tiles, dtype=jnp.int32) * tm
    tile_expert = jnp.minimum(jnp.sum((tile_start[:, None] >= ends[None, :]).astype(jnp.int32), axis=1),
                              N_EXPERTS - 1).astype(jnp.int32)
    used_end = ends - padded + counts
    tile_rows = jnp.clip(used_end[tile_expert] - tile_start, 0, tm).astype(jnp.int32)
    offsets = [0]
    for s in sizes + sizes:
        offsets.append(offsets[-1] + s)
    pieces = [slot[offsets[j]:offsets[j + 1]] for j in range(2 * len(sizes))]
    slots = [(pieces[g], pieces[len(sizes) + g]) for g in range(len(sizes))]
    return slots, tile_expert, tile_rows, n_tiles * tm


def _final_norm_kernel(x_ref, g_ref, o_ref):
    o_ref[...] = _rms(x_ref[...], g_ref[...])


def _final_norm(x, g, tm):
    m, d = x.shape
    row = pl.BlockSpec((tm, d), lambda i: (i, 0))
    return pl.pallas_call(
        _final_norm_kernel, grid=(m // tm,), in_specs=[row, _full(g.shape)], out_specs=row,
        out_shape=jax.ShapeDtypeStruct((m, d), F32), compiler_params=_params("parallel"),
        name="final_norm",
    )(x, g)


def _row_tile(m, want):
    t = min(m, want)
    assert m % t == 0, (m, t)
    return t


def kernel(x_prompt, x_sample, cache_k, cache_v, cache_logf, cache_conv, cache_mem_k, cache_mem_v, page_table, mem_prompt, g_mix, w_in, b_fgate, w_dw, b_dw, g_cln, b_cln, g_attn_out, g_conv_out, w_out, g_cross, g_mem, w_cq, w_ckv, w_co, g_ffn, w_ff_gu, w_ff_down, w_router, b_router, w_e_gu, w_e_down, g_final):
    nb, seq, d = x_prompt.shape
    nd, t_new, _ = x_sample.shape
    depth = w_in.shape[0]
    a = ATTN_WIDTH
    cw = d - a
    n_mem = mem_prompt.shape[1]
    n_phys, page = cache_k.shape[1], cache_k.shape[2]
    hist = CONV_K - 1
    vec = lambda v: v.reshape(1, -1).astype(F32)
    pad_lanes = lambda v, fill=0.0: jnp.pad(v, [(0, 0)] * (v.ndim - 1) + [(0, LANES - v.shape[-1])],
                                            constant_values=fill)

    xs = [x_prompt.reshape(nb * seq, d), x_sample.reshape(nd * t_new, d)]
    mem2 = mem_prompt.reshape(nb * n_mem, d)
    cache_kt = cache_k.transpose(0, 1, 3, 4, 2).reshape(depth * n_phys, a, page)
    cache_vt = cache_v.transpose(0, 1, 3, 4, 2).reshape(depth * n_phys, a, page)
    cache_lft = cache_logf.astype(F32).transpose(0, 1, 3, 2).reshape(depth * n_phys, N_HEADS, page)
    mem_k_rows = cache_mem_k.astype(F32).reshape(depth * nd, n_mem * X_HEADS, X_HEAD_DIM)
    mem_v_rows = cache_mem_v.astype(F32).reshape(depth * nd, n_mem * X_HEADS, X_HEAD_DIM)
    outs = {k: [] for k in ("lfp", "cvp", "mkp", "mvp", "ks", "vs", "lfs", "cvs")}
    kv_stack = (None, None)

    for l in range(depth):
        wqkv = w_in[l][:, :3 * a].astype(BF16)
        wf = pad_lanes(w_in[l][:, 3 * a:3 * a + N_HEADS]).astype(BF16)
        wglu = w_in[l][:, 3 * a + N_HEADS:].astype(BF16)
        bf = pad_lanes(vec(b_fgate[l]))
        wa = w_out[l][:a].astype(BF16)
        wc = w_out[l][a:].astype(BF16)
        wq = w_cq[l].astype(BF16)
        wo = w_co[l].astype(BF16)
        wk_mem = w_ckv[l][:, :X_WIDTH].astype(BF16)
        wv_mem = w_ckv[l][:, X_WIDTH:].astype(BF16)
        conv_args = (w_dw[l].astype(F32), vec(b_dw[l]), vec(g_cln[l]), vec(b_cln[l]), vec(g_conv_out[l]))
        last = l == depth - 1

        mk_p, mv_p = _memory_kv(mem2, vec(g_mem[l]), wk_mem, wv_mem, _row_tile(nb * n_mem, 512))
        outs["mkp"].append(mk_p.reshape(nb, n_mem, X_HEADS, X_HEAD_DIM))
        outs["mvp"].append(mv_p.reshape(nb, n_mem, X_HEADS, X_HEAD_DIM))
        mem_rows = n_mem * X_HEADS
        mems = [(mk_p.reshape(nb, mem_rows, X_HEAD_DIM), mv_p.reshape(nb, mem_rows, X_HEAD_DIM), 0),
                (mem_k_rows, mem_v_rows, l * nd)]

        tiles = [_row_tile(x.shape[0], 512) for x in xs]
        g_f = vec(g_ffn[l])
        router = None
        if l % 2 == 1:
            w_r = pad_lanes(w_router[l // 2].astype(F32))
            w_r_hi = w_r.astype(BF16)
            router = (g_f, w_r_hi, (w_r - w_r_hi.astype(F32)).astype(BF16), pad_lanes(vec(b_router[l // 2])))
        tops = [None, None]
        for grp in range(2):
            x = xs[grp]
            m = x.shape[0]
            tm = tiles[grp]
            stacked = (seq, kv_stack[0], kv_stack[1]) if grp == 0 else None
            q, k, v, kb, vb, lf_pad, u = _proj_in(x, vec(g_mix[l]), wqkv, wf, wglu, bf, tm, stacked)
            lf = lf_pad[:, :N_HEADS]
            if grp == 0:
                c = _cumsum_logf(lf_pad, nb, seq)
                o_attn = _fox_prompt(q, kb, vb, c, nb, seq, _row_tile(seq, 512), _row_tile(seq, 512))
                u3 = u.reshape(nb, seq, cw)
                o_conv = _conv_prompt(u3, *conv_args, _row_tile(seq, 512)).reshape(m, cw)
                kv_stack = (k, v)
                outs["lfp"].append(lf.reshape(nb, seq, N_HEADS))
                outs["cvp"].append(u3[:, seq - hist:])
                n_grp, t_grp = nb, seq
            else:
                new_t = lambda z: jnp.pad(z.reshape(nd, t_new, -1).transpose(0, 2, 1),
                                          ((0, 0), (0, 0), (0, page - t_new)))
                o_attn = _fox_sample(page_table, l * n_phys, q, new_t(k).reshape(nd * a, page),
                                     new_t(v).reshape(nd * a, page), new_t(lf),
                                     cache_kt, cache_vt, cache_lft, t_new)
                u_hist = jnp.concatenate([cache_conv[l].astype(F32), u.reshape(nd, t_new, cw)], axis=1)
                o_conv = _conv_sample(u_hist, *conv_args, t_new).reshape(m, cw)
                outs["ks"].append(k.reshape(nd, t_new, N_HEADS, HEAD_DIM))
                outs["vs"].append(v.reshape(nd, t_new, N_HEADS, HEAD_DIM))
                outs["lfs"].append(lf.reshape(nd, t_new, N_HEADS))
                outs["cvs"].append(u_hist[:, -hist:])
                n_grp, t_grp = nd, t_new
            xs[grp], tops[grp] = _mix_cross(x, o_attn, o_conv, vec(g_attn_out[l]), wa, wc, vec(g_cross[l]), wq,
                                            mems[grp][0], mems[grp][1], wo, n_grp, mems[grp][2],
                                            _row_tile(t_grp, 512), router)

        if l % 2 == 0:
            wgu_d, wd_d = w_ff_gu[l // 2].astype(BF16), w_ff_down[l // 2].astype(BF16)
            xs = [_ffn(x, g_f, wgu_d, wd_d, tm, 256) for x, tm in zip(xs, tiles)]
            if last:
                xs = [_final_norm(x, vec(g_final), tm) for x, tm in zip(xs, tiles)]
        else:
            e = l // 2
            slots, tile_expert, tile_rows, n_slots = _moe_routing(tops, MOE_TILE)
            slot_rows = jnp.zeros((n_slots * SUBLANES, LANES), F32)
            for x, (s1, s2) in zip(xs, slots):
                slot_rows = _moe_dispatch(s1, s2, x, g_f, slot_rows, _row_tile(x.shape[0], 256))
            ys = _moe_sparse(tile_expert, tile_rows, slot_rows, w_e_gu[e], w_e_down[e], MOE_TILE, 512)
            xs = [_moe_combine(s1, s2, x, top, vec(g_final), ys, last, _row_tile(x.shape[0], 256))
                  for x, top, (s1, s2) in zip(xs, tops, slots)]

    st = lambda key: jnp.stack(outs[key])
    kv_prompt = lambda z: z.reshape(depth, nb, N_HEADS, HEAD_DIM, seq).transpose(0, 1, 4, 2, 3)
    return (xs[0].reshape(nb, seq, d), xs[1].reshape(nd, t_new, d),
            kv_prompt(kv_stack[0]), kv_prompt(kv_stack[1]), st("lfp"), st("cvp"), st("mkp"), st("mvp"),
            st("ks"), st("vs"), st("lfs"), st("cvs"))
```

```python
import functools

import jax
import jax.numpy as jnp
from jax import lax
from jax.experimental import pallas as pl
from jax.experimental.pallas import tpu as pltpu

EPS = 1e-6
N_HEADS = 8
HEAD_DIM = 64
ATTN_WIDTH = N_HEADS * HEAD_DIM
CONV_K = 31
CONV_HALO = 32
X_HEADS = 4
X_HEAD_DIM = 128
X_WIDTH = X_HEADS * X_HEAD_DIM
N_EXPERTS = 8
LANES = 128
SUBLANES = 8
NEG = -1e30
LOG2E = 1.4426950408889634
VMEM_LIMIT_BYTES = 56 * 1024 * 1024
PAGES_PER_STEP = 32
MOE_TILE = 1024

F32 = jnp.float32
BF16 = jnp.bfloat16
HIGHEST = lax.Precision.HIGHEST


def _params(*sem):
    return pltpu.CompilerParams(dimension_semantics=sem, vmem_limit_bytes=VMEM_LIMIT_BYTES)


def _rms(x, g):
    return x * lax.rsqrt(jnp.mean(x * x, axis=-1, keepdims=True) + EPS) * g


def _sigmoid(x):
    return 1.0 / (1.0 + jnp.exp(-x))


def _dot(a, b):
    return jnp.dot(a, b, preferred_element_type=F32)


def _dot_nt(a, b):
    return lax.dot_general(a, b, (((1,), (1,)), ((), ())), preferred_element_type=F32)


def _full(shape):
    nd = len(shape)
    return pl.BlockSpec(shape, lambda *_: (0,) * nd)


def _proj_in_kernel(x_ref, g_ref, wqkv_ref, wf_ref, wglu_ref, bf_ref,
                    q_ref, k_ref, v_ref, kb_ref, vb_ref, lf_ref, u_ref, *, transposed):
    a = ATTN_WIDTH
    h = _rms(x_ref[...], g_ref[...]).astype(BF16)
    qkv = _dot(h, wqkv_ref[...])
    q_ref[...] = (qkv[:, :a] * (HEAD_DIM ** -0.5 * LOG2E)).astype(BF16)
    k = qkv[:, a:2 * a]
    v = qkv[:, 2 * a:]
    if transposed:
        k_ref[0, 0] = k.T
        v_ref[0, 0] = v.T
    else:
        k_ref[...] = k
        v_ref[...] = v
    kb_ref[...] = k.astype(BF16)
    vb_ref[...] = v.astype(BF16)
    z = _dot(h, wf_ref[...]) + bf_ref[...]
    lf_ref[...] = jnp.minimum(z, 0.0) - jnp.log1p(jnp.exp(-jnp.abs(z)))
    glu = _dot(h, wglu_ref[...])
    c = glu.shape[1] // 2
    u_ref[...] = glu[:, :c] * _sigmoid(glu[:, c:])


def _proj_in_carry_kernel(inner, x_ref, g_ref, wqkv_ref, wf_ref, wglu_ref, bf_ref, k_prev, v_prev,
                          q_ref, k_ref, v_ref, *out_refs):
    n_prev = k_prev.shape[0]
    k_ref[0:n_prev] = k_prev[...]
    v_ref[0:n_prev] = v_prev[...]
    inner(x_ref, g_ref, wqkv_ref, wf_ref, wglu_ref, bf_ref, q_ref, k_ref.at[n_prev:], v_ref.at[n_prev:], *out_refs)


def _proj_in(x, g, wqkv, wf, wglu, bf, tm, stacked=None):
    m, d = x.shape
    a = ATTN_WIDTH
    c = wglu.shape[1] // 2
    row = lambda w: pl.BlockSpec((tm, w), lambda i: (i, 0))
    args = [x, g, wqkv, wf, wglu, bf]
    in_specs = [row(d), _full((1, d)), _full(wqkv.shape), _full(wf.shape), _full(wglu.shape), _full((1, LANES))]
    if stacked is None:
        kv_spec, kv_shape = row(a), jax.ShapeDtypeStruct((m, a), F32)
        kernel_fn = functools.partial(_proj_in_kernel, transposed=False)
    else:
        seq_len, k_prev, v_prev = stacked
        per = seq_len // tm
        n_prev = 0 if k_prev is None else k_prev.shape[0]
        slabs = lambda n: pl.BlockSpec((n, 1, a, tm), lambda i: (0, i // per, 0, i % per))
        kv_spec = slabs(n_prev + 1)
        kv_shape = jax.ShapeDtypeStruct((n_prev + 1, m // seq_len, a, seq_len), F32)
        kernel_fn = functools.partial(_proj_in_kernel, transposed=True)
        if n_prev:
            args += [k_prev, v_prev]
            in_specs += [slabs(n_prev)] * 2
            kernel_fn = functools.partial(_proj_in_carry_kernel, kernel_fn)
    return pl.pallas_call(
        kernel_fn,
        grid=(m // tm,),
        in_specs=in_specs,
        out_specs=[row(a), kv_spec, kv_spec, row(a), row(a), row(LANES), row(c)],
        out_shape=[jax.ShapeDtypeStruct((m, a), BF16), kv_shape, kv_shape, jax.ShapeDtypeStruct((m, a), BF16),
                   jax.ShapeDtypeStruct((m, a), BF16), jax.ShapeDtypeStruct((m, LANES), F32),
                   jax.ShapeDtypeStruct((m, c), F32)],
        compiler_params=_params("parallel"),
        name="proj_in",
    )(*args)


def _cumsum_kernel(lf_ref, c_ref):
    s = lf_ref.shape[0]
    lft = lf_ref[...].T[:N_HEADS, :]
    r = lax.broadcasted_iota(jnp.int32, (LANES, LANES), 0)
    cidx = lax.broadcasted_iota(jnp.int32, (LANES, LANES), 1)
    tri = (r <= cidx).astype(F32)
    carry = jnp.zeros((N_HEADS, 1), F32)
    for b in range(s // LANES):
        cb = jnp.dot(lft[:, b * LANES:(b + 1) * LANES], tri, precision=HIGHEST,
                     preferred_element_type=F32) + carry
        c_ref[0, :, b * LANES:(b + 1) * LANES] = cb * LOG2E
        carry = cb[:, LANES - 1:LANES]


def _cumsum_logf(lf_pad, nb, s):
    return pl.pallas_call(
        _cumsum_kernel,
        grid=(nb,),
        in_specs=[pl.BlockSpec((s, LANES), lambda b: (b, 0))],
        out_specs=pl.BlockSpec((1, N_HEADS, s), lambda b: (b, 0, 0)),
        out_shape=jax.ShapeDtypeStruct((nb, N_HEADS, s), F32),
        compiler_params=_params("parallel"),
        name="cumsum_logf",
    )(lf_pad)


def _fox_prompt_kernel(q_ref, k_ref, v_ref, c_ref, o_ref, m_scr, l_scr, acc_scr, *, tq, tk):
    i = pl.program_id(2)
    lane = lax.broadcasted_iota(jnp.int32, (1, LANES), 1)
    q = q_ref[...]
    zero = jnp.zeros_like(q)
    qh = (jnp.where(lane < HEAD_DIM, q, zero), jnp.where(lane >= HEAD_DIM, q, zero))
    m_scr[...] = jnp.full(m_scr.shape, NEG, F32)
    l_scr[...] = jnp.zeros(l_scr.shape, F32)
    acc_scr[...] = jnp.zeros(acc_scr.shape, F32)
    row = lax.broadcasted_iota(jnp.int32, (tq, tk), 0)
    col = lax.broadcasted_iota(jnp.int32, (tq, tk), 1)
    ratio = tq // tk

    def step(j, diag):
        start = pl.multiple_of(j * tk, tk)
        kb = k_ref[pl.ds(start, tk), :]
        vb = v_ref[pl.ds(start, tk), :]
        cs = c_ref[:, j]
        for hh in range(2):
            s = _dot_nt(qh[hh], kb) - cs[hh]
            if diag is not None:
                s = jnp.where(col + diag * tk <= row, s, NEG)
            m_prev = m_scr[hh]
            m_new = jnp.maximum(m_prev, jnp.max(s, axis=-1, keepdims=True))
            alpha = jnp.exp2(m_prev - m_new)
            p = jnp.exp2(s - jnp.concatenate([m_new] * (tk // LANES), axis=1))
            l_scr[hh] = alpha * l_scr[hh] + jnp.sum(p, axis=-1, keepdims=True)
            acc_scr[hh] = alpha * acc_scr[hh] + _dot(p.astype(BF16), vb)
            m_scr[hh] = m_new

    def body(j, carry):
        step(j, None)
        return carry

    lax.fori_loop(0, i * ratio, body, 0)
    for dg in range(ratio):
        step(i * ratio + dg, dg)
    o_ref[...] = jnp.where(lane < HEAD_DIM, acc_scr[0] / l_scr[0], acc_scr[1] / l_scr[1])


def _fox_prompt(q, kb, vb, c, nb, s, tq, tk):
    m = q.shape[0]
    nt = s // tq
    nk = s // tk
    c4 = c.reshape(nb * N_HEADS, nk, 1, tk)
    pairs = N_HEADS // 2
    return pl.pallas_call(
        functools.partial(_fox_prompt_kernel, tq=tq, tk=tk),
        grid=(nb, pairs, nt),
        in_specs=[pl.BlockSpec((tq, LANES), lambda b, p, i: (b * nt + i, p)),
                  pl.BlockSpec((s, LANES), lambda b, p, i: (b, p)),
                  pl.BlockSpec((s, LANES), lambda b, p, i: (b, p)),
                  pl.BlockSpec((2, nk, 1, tk), lambda b, p, i: (b * pairs + p, 0, 0, 0))],
        out_specs=pl.BlockSpec((tq, LANES), lambda b, p, i: (b * nt + i, p)),
        out_shape=jax.ShapeDtypeStruct((m, ATTN_WIDTH), F32),
        scratch_shapes=[pltpu.VMEM((2, tq, LANES), F32), pltpu.VMEM((2, tq, LANES), F32),
                        pltpu.VMEM((2, tq, LANES), F32)],
        compiler_params=_params("parallel", "parallel", "arbitrary"),
        name="fox_prompt",
    )(q, kb, vb, c4)


def _fox_sample_kernel(pt_ref, q_ref, kn_ref, vn_ref, lfn_ref, *rest, t_new, n_chunks):
    pg = PAGES_PER_STEP
    k_refs, v_refs, lf_refs = rest[:pg], rest[pg:2 * pg], rest[2 * pg:3 * pg]
    o_ref, m_scr, l_scr, acc_scr, carry_scr = rest[3 * pg:]
    del pt_ref
    j = pl.program_id(1)
    rows = N_HEADS * t_new
    width = ATTN_WIDTH
    page = lf_refs[0].shape[2]

    @pl.when(j == 0)
    def _():
        m_scr[...] = jnp.full(m_scr.shape, NEG, F32)
        l_scr[...] = jnp.zeros(l_scr.shape, F32)
        acc_scr[...] = jnp.zeros(acc_scr.shape, F32)
        carry_scr[...] = jnp.zeros(carry_scr.shape, F32)

    q = q_ref[...]
    q_rows = jnp.concatenate([q] * N_HEADS, axis=0)
    rr = lax.broadcasted_iota(jnp.int32, (rows, width), 0)
    ll = lax.broadcasted_iota(jnp.int32, (rows, width), 1)
    head_mask = (rr // t_new) == (ll // HEAD_DIM)
    q_rows = jnp.where(head_mask, q_rows, jnp.zeros_like(q_rows))

    def per_query_rows(x):
        return jnp.concatenate([jnp.broadcast_to(x[h:h + 1], (t_new, x.shape[1])) for h in range(N_HEADS)],
                               axis=0)

    def attend(kts, vts, biases):
        s = jnp.concatenate([_dot(q_rows, kt) + b for kt, b in zip(kts, biases)], axis=1)
        m_prev = m_scr[...]
        m_new = jnp.maximum(m_prev, jnp.max(s, axis=-1, keepdims=True))
        alpha = jnp.exp2(m_prev - m_new)
        p = jnp.exp2(s - m_new)
        l_scr[...] = alpha * l_scr[...] + jnp.sum(p, axis=-1, keepdims=True)
        p = p.astype(BF16)
        pv = _dot_nt(p[:, :page], vts[0])
        for g in range(1, len(vts)):
            pv = pv + _dot_nt(p[:, g * page:(g + 1) * page], vts[g])
        acc_scr[...] = alpha * acc_scr[...] + pv
        m_scr[...] = m_new

    lf = jnp.concatenate([lf_refs[g][0] for g in range(pg)], axis=0)
    lane_all = lax.broadcasted_iota(jnp.int32, lf.shape, 1)
    suffix = lf
    shift = 1
    while shift < page:
        nxt = pltpu.roll(suffix, page - shift, axis=1)
        suffix = suffix + jnp.where(lane_all + shift < page, nxt, 0.0)
        shift *= 2
    within = suffix - lf
    totals = jnp.sum(lf, axis=-1, keepdims=True)
    later = carry_scr[...]
    biases = [None] * pg
    for g in reversed(range(pg)):
        biases[g] = per_query_rows((within[g * N_HEADS:(g + 1) * N_HEADS] + later) * LOG2E)
        later = later + totals[g * N_HEADS:(g + 1) * N_HEADS]
    carry_scr[...] = later
    attend([k_refs[g][0].astype(BF16) for g in range(pg)], [v_refs[g][0].astype(BF16) for g in range(pg)], biases)
    lane = lax.broadcasted_iota(jnp.int32, (N_HEADS, page), 1)

    @pl.when(j == n_chunks - 1)
    def _():
        c_new = lfn_ref[0]
        shift = 1
        while shift < t_new:
            c_new = c_new + jnp.where(lane >= shift, pltpu.roll(c_new, shift, axis=1), 0.0)
            shift *= 2
        br = lax.broadcasted_iota(jnp.int32, (rows, page), 0)
        bc = lax.broadcasted_iota(jnp.int32, (rows, page), 1)
        bias = jnp.where(bc <= (br % t_new), per_query_rows(c_new * (-LOG2E)), NEG)
        attend([kn_ref[...].astype(BF16)], [vn_ref[...].astype(BF16)], [bias])
        o = jnp.where(head_mask, acc_scr[...] / l_scr[...], 0.0)
        out = o[0:t_new]
        for h in range(1, N_HEADS):
            out = out + o[h * t_new:(h + 1) * t_new]
        o_ref[...] = out


def _fox_sample(page_table, base, q, kt_new, vt_new, lft_new, cache_kt, cache_vt, cache_lft, t_new):
    n, n_pages = page_table.shape
    pg = PAGES_PER_STEP
    n_chunks = n_pages // pg
    assert n_pages % pg == 0
    width, page = cache_kt.shape[1], cache_kt.shape[2]
    rows = N_HEADS * t_new

    def page_spec(shape, g):
        return pl.BlockSpec(shape, lambda b, j, pt: (base + pt[b, (n_chunks - 1 - j) * pg + g], 0, 0))

    in_specs = ([pl.BlockSpec((t_new, width), lambda b, j, pt: (b, 0)),
                 pl.BlockSpec((width, page), lambda b, j, pt: (b, 0)),
                 pl.BlockSpec((width, page), lambda b, j, pt: (b, 0)),
                 pl.BlockSpec((1, N_HEADS, page), lambda b, j, pt: (b, 0, 0))]
                + [page_spec((1, width, page), g) for g in range(pg)]
                + [page_spec((1, width, page), g) for g in range(pg)]
                + [page_spec((1, N_HEADS, page), g) for g in range(pg)])
    return pl.pallas_call(
        functools.partial(_fox_sample_kernel, t_new=t_new, n_chunks=n_chunks),
        grid_spec=pltpu.PrefetchScalarGridSpec(
            num_scalar_prefetch=1,
            grid=(n, n_chunks),
            in_specs=in_specs,
            out_specs=pl.BlockSpec((t_new, width), lambda b, j, pt: (b, 0)),
            scratch_shapes=[pltpu.VMEM((rows, 1), F32), pltpu.VMEM((rows, 1), F32),
                            pltpu.VMEM((rows, width), F32), pltpu.VMEM((N_HEADS, 1), F32)]),
        out_shape=jax.ShapeDtypeStruct((n * t_new, width), F32),
        compiler_params=_params("parallel", "arbitrary"),
        name="fox_sample",
    )(page_table, q, kt_new, vt_new, lft_new, *([cache_kt] * pg), *([cache_vt] * pg), *([cache_lft] * pg))


def _conv_tail(z, b_ref, gl_ref, bl_ref, gco_ref):
    z = z + b_ref[...]
    zc = z - jnp.mean(z, axis=-1, keepdims=True)
    y = zc * lax.rsqrt(jnp.mean(zc * zc, axis=-1, keepdims=True) + EPS) * gl_ref[...] + bl_ref[...]
    y = y * _sigmoid(y)
    return _rms(y, gco_ref[...]).astype(BF16)


def _conv_prompt_kernel(u_ref, halo_ref, w_ref, b_ref, gl_ref, bl_ref, gco_ref, o_ref, xs_ref, sh_ref, *, ts, rc):
    i = pl.program_id(1)
    halo = halo_ref[0]
    xs_ref[0:CONV_HALO, :] = jnp.where(i > 0, halo, jnp.zeros_like(halo))
    xs_ref[CONV_HALO:CONV_HALO + ts, :] = u_ref[0]
    n_sh = sh_ref.shape[1]
    for r in range(1, SUBLANES):
        sh_ref[r - 1] = xs_ref[r:r + n_sh, :]
    off = CONV_HALO - (CONV_K - 1)
    width = u_ref.shape[2]
    for c in range(ts // rc):
        acc = jnp.zeros((rc // SUBLANES, SUBLANES, width), F32)
        for j in range(CONV_K):
            q, r = divmod(off + j, SUBLANES)
            lo = c * rc + q * SUBLANES
            rows = xs_ref[lo:lo + rc, :] if r == 0 else sh_ref[r - 1, lo:lo + rc, :]
            acc = acc + w_ref[j][None] * rows.reshape(rc // SUBLANES, SUBLANES, width)
        o_ref[0, c * rc:(c + 1) * rc, :] = _conv_tail(acc.reshape(rc, width), b_ref, gl_ref, bl_ref, gco_ref)


def _conv_prompt(u3, w, b, gl, bl, gco, ts):
    nb, s, c = u3.shape
    hb = ts // CONV_HALO
    vec = _full((1, c))
    w = jnp.broadcast_to(w[:, None, :], (w.shape[0], SUBLANES, c))
    return pl.pallas_call(
        functools.partial(_conv_prompt_kernel, ts=ts, rc=32),
        grid=(nb, s // ts),
        in_specs=[pl.BlockSpec((1, ts, c), lambda n, i: (n, i, 0)),
                  pl.BlockSpec((1, CONV_HALO, c), lambda n, i: (n, jnp.maximum(i * hb - 1, 0), 0)),
                  _full(w.shape), vec, vec, vec, vec],
        out_specs=pl.BlockSpec((1, ts, c), lambda n, i: (n, i, 0)),
        out_shape=jax.ShapeDtypeStruct((nb, s, c), BF16),
        scratch_shapes=[pltpu.VMEM((CONV_HALO + ts, c), F32),
                        pltpu.VMEM((SUBLANES - 1, CONV_HALO + ts - SUBLANES, c), F32)],
        compiler_params=_params("parallel", "parallel"),
        name="conv_prompt",
    )(u3, u3, w, b, gl, bl, gco)


def _conv_sample_kernel(uh_ref, w_ref, b_ref, gl_ref, bl_ref, gco_ref, o_ref, *, t_new):
    def one_sequence(n, carry):
        acc = jnp.zeros((t_new, uh_ref.shape[2]), F32)
        for j in range(CONV_K):
            acc = acc + w_ref[j:j + 1, :] * uh_ref[n, j:j + t_new, :]
        o_ref[n] = _conv_tail(acc, b_ref, gl_ref, bl_ref, gco_ref)
        return carry

    lax.fori_loop(0, uh_ref.shape[0], one_sequence, 0)


def _conv_sample(u_hist, w, b, gl, bl, gco, t_new):
    n, hl, c = u_hist.shape
    vec = _full((1, c))
    return pl.pallas_call(
        functools.partial(_conv_sample_kernel, t_new=t_new),
        grid=(1,),
        in_specs=[_full(u_hist.shape), _full(w.shape), vec, vec, vec, vec],
        out_specs=_full((n, t_new, c)),
        out_shape=jax.ShapeDtypeStruct((n, t_new, c), BF16),
        compiler_params=_params("arbitrary"),
        name="conv_sample",
    )(u_hist, w, b, gl, bl, gco)


def _memory_kv_kernel(mem_ref, g_ref, wk_ref, wv_ref, mk_ref, mv_ref):
    h = _rms(mem_ref[...], g_ref[...]).astype(BF16)
    rows = mem_ref.shape[0]
    for ref, w_ref in ((mk_ref, wk_ref), (mv_ref, wv_ref)):
        kv = _dot(h, w_ref[...])
        for hh in range(X_HEADS):
            ref[pl.ds(hh, rows, stride=X_HEADS), :] = kv[:, hh * X_HEAD_DIM:(hh + 1) * X_HEAD_DIM]


def _memory_kv(mem, g, wk, wv, tm):
    m, d = mem.shape
    out = pl.BlockSpec((tm * X_HEADS, X_HEAD_DIM), lambda i: (i, 0))
    return pl.pallas_call(
        _memory_kv_kernel,
        grid=(m // tm,),
        in_specs=[pl.BlockSpec((tm, d), lambda i: (i, 0)), _full(g.shape), _full(wk.shape), _full(wv.shape)],
        out_specs=[out, out],
        out_shape=[jax.ShapeDtypeStruct((m * X_HEADS, X_HEAD_DIM), F32)] * 2,
        compiler_params=_params("parallel"),
        name="memory_kv",
    )(mem, g, wk, wv)


def _mix_cross_kernel(x_ref, oa_ref, oc_ref, ga_ref, wa_ref, wc_ref, g_ref, wq_ref, mk_ref, mv_ref, wo_ref,
                      *rest, routed):
    o_ref = rest[4] if routed else rest[0]
    a = _rms(oa_ref[...], ga_ref[...]).astype(BF16)
    x = x_ref[...] + _dot(a, wa_ref[...]) + _dot(oc_ref[...], wc_ref[...])
    h = _rms(x, g_ref[...]).astype(BF16)
    q = (_dot(h, wq_ref[...]) * (X_HEAD_DIM ** -0.5)).astype(BF16)
    n_mem = mk_ref.shape[1] // X_HEADS
    outs = []
    for hh in range(X_HEADS):
        head = pl.ds(hh, n_mem, stride=X_HEADS)
        s = _dot_nt(q[:, hh * X_HEAD_DIM:(hh + 1) * X_HEAD_DIM], mk_ref[0, head, :].astype(BF16))
        p = jnp.exp(s - jnp.max(s, axis=-1, keepdims=True))
        o = _dot(p.astype(BF16), mv_ref[0, head, :].astype(BF16)) / jnp.sum(p, axis=-1, keepdims=True)
        outs.append(o.astype(BF16))
    x = x + _dot(jnp.concatenate(outs, axis=1), wo_ref[...])
    o_ref[...] = x
    if routed:
        gf_ref, whi_ref, wlo_ref, br_ref, _, top_ref = rest
        top_ref[...] = _route(x, gf_ref[...], (whi_ref[...], wlo_ref[...]), br_ref[...])


def _mix_cross(x, o_attn, o_conv, ga, wa, wc, g, wq, mk, mv, wo, nb, base, tm, router=None):
    m, d = x.shape
    per = (m // nb) // tm
    mem_rows = mk.shape[1]
    rows = lambda w: pl.BlockSpec((tm, w), lambda b, i: (b * per + i, 0))
    mem = pl.BlockSpec((1, mem_rows, X_HEAD_DIM), lambda b, i: (base + b, 0, 0))
    args = [x, o_attn, o_conv, ga, wa, wc, g, wq, mk, mv, wo]
    in_specs = [rows(d), rows(o_attn.shape[1]), rows(o_conv.shape[1]), _full(ga.shape), _full(wa.shape),
                _full(wc.shape), _full(g.shape), _full(wq.shape), mem, mem, _full(wo.shape)]
    out_specs, out_shape = [rows(d)], [jax.ShapeDtypeStruct((m, d), F32)]
    if router is not None:
        args += list(router)
        in_specs += [_full(r.shape) for r in router]
        out_specs.append(rows(LANES))
        out_shape.append(jax.ShapeDtypeStruct((m, LANES), F32))
    res = pl.pallas_call(
        functools.partial(_mix_cross_kernel, routed=router is not None),
        grid=(nb, per),
        in_specs=in_specs,
        out_specs=out_specs,
        out_shape=out_shape,
        compiler_params=_params("parallel", "parallel"),
        name="mix_cross",
    )(*args)
    return (res[0], res[1]) if router is not None else (res[0], None)


def _ffn_kernel(x_ref, g_ref, wgu_ref, wd_ref, o_ref, acc_ref, *, tf):
    x = x_ref[...]
    h = _rms(x, g_ref[...]).astype(BF16)
    f = wd_ref.shape[0]
    acc_ref[...] = x
    for c in range(f // tf):
        gt = _dot(h, wgu_ref[:, c * tf:(c + 1) * tf])
        up = _dot(h, wgu_ref[:, f + c * tf:f + (c + 1) * tf])
        act = (gt * _sigmoid(gt) * up).astype(BF16)
        acc_ref[...] += _dot(act, wd_ref[c * tf:(c + 1) * tf, :])
    o_ref[...] = acc_ref[...]


def _ffn(x, g, wgu, wd, tm, tf):
    m, d = x.shape
    row = pl.BlockSpec((tm, d), lambda i: (i, 0))
    resident = lambda shape: pl.BlockSpec(shape, lambda i: (0, 0), pipeline_mode=pl.Buffered(1))
    return pl.pallas_call(
        functools.partial(_ffn_kernel, tf=tf),
        grid=(m // tm,),
        in_specs=[row, _full(g.shape), resident(wgu.shape), resident(wd.shape)],
        out_specs=row,
        out_shape=jax.ShapeDtypeStruct((m, d), F32),
        scratch_shapes=[pltpu.VMEM((tm, d), F32)],
        compiler_params=_params("parallel"),
        name="ffn_dense",
    )(x, g, wgu, wd)


def _route(x, g, wr, br):
    h = _rms(x, g)
    h_hi = h.astype(BF16)
    h_lo = (h - h_hi.astype(F32)).astype(BF16)
    w_hi, w_lo = wr
    logits = _dot(h_hi, w_hi) + _dot(h_hi, w_lo) + _dot(h_lo, w_hi) + br
    lane = lax.broadcasted_iota(jnp.int32, logits.shape, 1)
    logits = jnp.where(lane < N_EXPERTS, logits, NEG)
    p = jnp.exp(logits - jnp.max(logits, axis=-1, keepdims=True))
    p = p / jnp.sum(p, axis=-1, keepdims=True)
    v1 = jnp.max(p, axis=-1, keepdims=True)
    i1 = jnp.min(jnp.where(p == v1, lane, LANES), axis=-1, keepdims=True)
    rest = jnp.where(lane == i1, -1.0, p)
    v2 = jnp.max(rest, axis=-1, keepdims=True)
    i2 = jnp.min(jnp.where(rest == v2, lane, LANES), axis=-1, keepdims=True)
    tot = v1 + v2
    g1 = v1 / tot
    g2 = v2 / tot
    return jnp.where(lane == 0, i1.astype(F32),
                     jnp.where(lane == 1, i2.astype(F32), jnp.where(lane == 2, g1, jnp.where(lane == 3, g2, 0.0))))


def _to_tiles(ref, x):
    for a in range(SUBLANES):
        ref[pl.ds(a, x.shape[0], stride=SUBLANES), :] = x[:, a * LANES:(a + 1) * LANES]


def _from_tiles(ref):
    rows = ref.shape[0] // SUBLANES
    return jnp.concatenate([ref[pl.ds(a, rows, stride=SUBLANES), :] for a in range(SUBLANES)], axis=1)


def _tile_at(ref, slot):
    return ref.at[pl.ds(pl.multiple_of(slot * SUBLANES, SUBLANES), SUBLANES)]


def _moe_dispatch_kernel(s1_ref, s2_ref, x_ref, g_ref, xs_in, xs_hbm, stage, sem, *, td):
    del xs_in
    i = pl.program_id(0)
    _to_tiles(stage, _rms(x_ref[...], g_ref[...]))

    def issue(r, carry):
        src = _tile_at(stage, r)
        pltpu.make_async_copy(src, _tile_at(xs_hbm, s1_ref[i * td + r]), sem).start(priority=0)
        pltpu.make_async_copy(src, _tile_at(xs_hbm, s2_ref[i * td + r]), sem).start(priority=1)
        return carry

    lax.fori_loop(0, td, issue, 0, unroll=8)
    for _ in range(2):
        pltpu.make_async_copy(stage, xs_hbm.at[pl.ds(0, td * SUBLANES)], sem).wait()


def _moe_dispatch(slot1, slot2, x, g, xs, td):
    m, d = x.shape
    return pl.pallas_call(
        functools.partial(_moe_dispatch_kernel, td=td),
        grid_spec=pltpu.PrefetchScalarGridSpec(
            num_scalar_prefetch=2,
            grid=(m // td,),
            in_specs=[pl.BlockSpec((td, d), lambda i, *_: (i, 0)), pl.BlockSpec((1, d), lambda i, *_: (0, 0)),
                      pl.BlockSpec(memory_space=pl.ANY)],
            out_specs=pl.BlockSpec(memory_space=pl.ANY),
            scratch_shapes=[pltpu.VMEM((td * SUBLANES, LANES), F32), pltpu.SemaphoreType.DMA(())]),
        out_shape=jax.ShapeDtypeStruct(xs.shape, F32),
        input_output_aliases={4: 0},
        compiler_params=_params("arbitrary"),
        name="moe_dispatch",
    )(slot1, slot2, x, g, xs)


def _moe_sparse_kernel(te_ref, rows_ref, x_ref, wg_ref, wu_ref, wd_ref, o_ref, h_scr, acc_scr):
    del te_ref
    i = pl.program_id(0)
    f = pl.program_id(1)
    last = pl.num_programs(1) - 1
    rows = rows_ref[i]
    tm = h_scr.shape[0]
    half = tm // 2

    @pl.when(f == 0)
    def _():
        acc_scr[...] = jnp.zeros(acc_scr.shape, F32)
        h_scr[...] = _from_tiles(x_ref).astype(BF16)

    def swiglu(n):
        h = h_scr[0:n]
        gt = _dot(h, wg_ref[0].astype(BF16))
        up = _dot(h, wu_ref[0].astype(BF16))
        act = (gt * _sigmoid(gt) * up).astype(BF16)
        acc_scr[0:n] += _dot(act, wd_ref[0].astype(BF16))

    @pl.when(rows > half)
    def _():
        swiglu(tm)

    @pl.when((rows > 0) & (rows <= half))
    def _():
        swiglu(half)

    @pl.when(f == last)
    def _():
        _to_tiles(o_ref, acc_scr[...])


def _moe_sparse(tile_expert, tile_rows, xs, wgu, wd, tm, tf):
    n_slots = xs.shape[0] // SUBLANES
    d, f_all = wd.shape[2], wd.shape[1]
    nf = f_all // tf

    def chunk(i, f, nv):
        return jnp.where(nv[i] > 0, f, nf - 1)

    tiles = pl.BlockSpec((tm * SUBLANES, LANES), lambda i, f, te, nv: (i, 0))
    return pl.pallas_call(
        _moe_sparse_kernel,
        grid_spec=pltpu.PrefetchScalarGridSpec(
            num_scalar_prefetch=2,
            grid=(n_slots // tm, nf),
            in_specs=[tiles,
                      pl.BlockSpec((1, d, tf), lambda i, f, te, nv: (te[i], 0, chunk(i, f, nv))),
                      pl.BlockSpec((1, d, tf), lambda i, f, te, nv: (te[i], 0, nf + chunk(i, f, nv))),
                      pl.BlockSpec((1, tf, d), lambda i, f, te, nv: (te[i], chunk(i, f, nv), 0))],
            out_specs=tiles,
            scratch_shapes=[pltpu.VMEM((tm, d), BF16), pltpu.VMEM((tm, d), F32)]),
        out_shape=jax.ShapeDtypeStruct(xs.shape, F32),
        compiler_params=_params("arbitrary", "arbitrary"),
        name="moe_sparse",
    )(tile_expert, tile_rows, xs, wgu, wgu, wd)


def _moe_combine_kernel(s1_ref, s2_ref, x_ref, top_ref, gfin_ref, ys_hbm, o_ref,
                        a0, b0, sem0, a1, b1, sem1, *, tc, final):
    i = pl.program_id(0)
    n = pl.num_programs(0)
    sets = ((a0, b0, sem0), (a1, b1, sem1))

    def start(step, which):
        buf_a, buf_b, sem = sets[which]

        def issue(r, carry):
            pltpu.make_async_copy(_tile_at(ys_hbm, s1_ref[step * tc + r]), _tile_at(buf_a, r),
                                  sem).start(priority=0)
            pltpu.make_async_copy(_tile_at(ys_hbm, s2_ref[step * tc + r]), _tile_at(buf_b, r),
                                  sem).start(priority=1)
            return carry

        lax.fori_loop(0, tc, issue, 0, unroll=8)

    def finish(which):
        buf_a, buf_b, sem = sets[which]
        pltpu.make_async_copy(ys_hbm.at[pl.ds(0, tc * SUBLANES)], buf_a, sem).wait()
        pltpu.make_async_copy(ys_hbm.at[pl.ds(0, tc * SUBLANES)], buf_b, sem).wait()
        top = top_ref[...]
        y = x_ref[...] + top[:, 2:3] * _from_tiles(buf_a) + top[:, 3:4] * _from_tiles(buf_b)
        o_ref[...] = _rms(y, gfin_ref[...]) if final else y

    @pl.when(i == 0)
    def _():
        start(0, 0)

    for which in range(2):
        @pl.when(i % 2 == which)
        def _(which=which):
            @pl.when(i + 1 < n)
            def _():
                start(i + 1, 1 - which)

            finish(which)


def _moe_combine(slot1, slot2, x, top, gfin, ys, final, tc):
    m, d = x.shape
    rows = lambda w: pl.BlockSpec((tc, w), lambda i, *_: (i, 0))
    return pl.pallas_call(
        functools.partial(_moe_combine_kernel, tc=tc, final=final),
        grid_spec=pltpu.PrefetchScalarGridSpec(
            num_scalar_prefetch=2,
            grid=(m // tc,),
            in_specs=[rows(d), rows(LANES), pl.BlockSpec((1, d), lambda i, *_: (0, 0)),
                      pl.BlockSpec(memory_space=pl.ANY)],
            out_specs=rows(d),
            scratch_shapes=[pltpu.VMEM((tc * SUBLANES, LANES), F32), pltpu.VMEM((tc * SUBLANES, LANES), F32),
                            pltpu.SemaphoreType.DMA(())] * 2),
        out_shape=jax.ShapeDtypeStruct((m, d), F32),
        compiler_params=_params("arbitrary"),
        name="moe_combine",
    )(slot1, slot2, x, top, gfin, ys)


def _moe_routing(tops, tm):
    sizes = [t.shape[0] for t in tops]
    experts = jnp.concatenate([t[:, k] for k in range(2) for t in tops]).astype(jnp.int32)
    onehot = (experts[:, None] == jnp.arange(N_EXPERTS, dtype=jnp.int32)[None, :]).astype(jnp.int32)
    rank = jnp.sum((jnp.cumsum(onehot, axis=0) - onehot) * onehot, axis=1)
    counts = jnp.sum(onehot, axis=0)
    padded = ((counts + tm - 1) // tm) * tm
    ends = jnp.cumsum(padded)
    slot = ((ends - padded)[experts] + rank).astype(jnp.int32)
    n_tiles = -(-2 * sum(sizes) // tm) + N_EXPERTS
    tile_start = jnp.arange(n_tiles, dtype=jnp.int32) * tm
    tile_expert = jnp.minimum(jnp.sum((tile_start[:, None] >= ends[None, :]).astype(jnp.int32), axis=1),
                              N_EXPERTS - 1).astype(jnp.int32)
    used_end = ends - padded + counts
    tile_rows = jnp.clip(used_end[tile_expert] - tile_start, 0, tm).astype(jnp.int32)
    offsets = [0]
    for s in sizes + sizes:
        offsets.append(offsets[-1] + s)
    pieces = [slot[offsets[j]:offsets[j + 1]] for j in range(2 * len(sizes))]
    slots = [(pieces[g], pieces[len(sizes) + g]) for g in range(len(sizes))]
    return slots, tile_expert, tile_rows, n_tiles * tm


def _final_norm_kernel(x_ref, g_ref, o_ref):
    o_ref[...] = _rms(x_ref[...], g_ref[...])


def _final_norm(x, g, tm):
    m, d = x.shape
    row = pl.BlockSpec((tm, d), lambda i: (i, 0))
    return pl.pallas_call(
        _final_norm_kernel, grid=(m // tm,), in_specs=[row, _full(g.shape)], out_specs=row,
        out_shape=jax.ShapeDtypeStruct((m, d), F32), compiler_params=_params("parallel"),
        name="final_norm",
    )(x, g)


def _row_tile(m, want):
    t = min(m, want)
    assert m % t == 0, (m, t)
    return t


def kernel(x_prompt, x_sample, cache_k, cache_v, cache_logf, cache_conv, cache_mem_k, cache_mem_v, page_table, mem_prompt, g_mix, w_in, b_fgate, w_dw, b_dw, g_cln, b_cln, g_attn_out, g_conv_out, w_out, g_cross, g_mem, w_cq, w_ckv, w_co, g_ffn, w_ff_gu, w_ff_down, w_router, b_router, w_e_gu, w_e_down, g_final):
    nb, seq, d = x_prompt.shape
    nd, t_new, _ = x_sample.shape
    depth = w_in.shape[0]
    a = ATTN_WIDTH
    cw = d - a
    n_mem = mem_prompt.shape[1]
    n_phys, page = cache_k.shape[1], cache_k.shape[2]
    hist = CONV_K - 1
    vec = lambda v: v.reshape(1, -1).astype(F32)
    pad_lanes = lambda v, fill=0.0: jnp.pad(v, [(0, 0)] * (v.ndim - 1) + [(0, LANES - v.shape[-1])],
                                            constant_values=fill)

    xs = [x_prompt.reshape(nb * seq, d), x_sample.reshape(nd * t_new, d)]
    mem2 = mem_prompt.reshape(nb * n_mem, d)
    cache_kt = cache_k.transpose(0, 1, 3, 4, 2).reshape(depth * n_phys, a, page)
    cache_vt = cache_v.transpose(0, 1, 3, 4, 2).reshape(depth * n_phys, a, page)
    cache_lft = cache_logf.astype(F32).transpose(0, 1, 3, 2).reshape(depth * n_phys, N_HEADS, page)
    mem_k_rows = cache_mem_k.astype(F32).reshape(depth * nd, n_mem * X_HEADS, X_HEAD_DIM)
    mem_v_rows = cache_mem_v.astype(F32).reshape(depth * nd, n_mem * X_HEADS, X_HEAD_DIM)
    outs = {k: [] for k in ("lfp", "cvp", "mkp", "mvp", "ks", "vs", "lfs", "cvs")}
    kv_stack = (None, None)

    for l in range(depth):
        wqkv = w_in[l][:, :3 * a].astype(BF16)
        wf = pad_lanes(w_in[l][:, 3 * a:3 * a + N_HEADS]).astype(BF16)
        wglu = w_in[l][:, 3 * a + N_HEADS:].astype(BF16)
        bf = pad_lanes(vec(b_fgate[l]))
        wa = w_out[l][:a].astype(BF16)
        wc = w_out[l][a:].astype(BF16)
        wq = w_cq[l].astype(BF16)
        wo = w_co[l].astype(BF16)
        wk_mem = w_ckv[l][:, :X_WIDTH].astype(BF16)
        wv_mem = w_ckv[l][:, X_WIDTH:].astype(BF16)
        conv_args = (w_dw[l].astype(F32), vec(b_dw[l]), vec(g_cln[l]), vec(b_cln[l]), vec(g_conv_out[l]))
        last = l == depth - 1

        mk_p, mv_p = _memory_kv(mem2, vec(g_mem[l]), wk_mem, wv_mem, _row_tile(nb * n_mem, 512))
        outs["mkp"].append(mk_p.reshape(nb, n_mem, X_HEADS, X_HEAD_DIM))
        outs["mvp"].append(mv_p.reshape(nb, n_mem, X_HEADS, X_HEAD_DIM))
        mem_rows = n_mem * X_HEADS
        mems = [(mk_p.reshape(nb, mem_rows, X_HEAD_DIM), mv_p.reshape(nb, mem_rows, X_HEAD_DIM), 0),
                (mem_k_rows, mem_v_rows, l * nd)]

        tiles = [_row_tile(x.shape[0], 512) for x in xs]
        g_f = vec(g_ffn[l])
        router = None
        if l % 2 == 1:
            w_r = pad_lanes(w_router[l // 2].astype(F32))
            w_r_hi = w_r.astype(BF16)
            router = (g_f, w_r_hi, (w_r - w_r_hi.astype(F32)).astype(BF16), pad_lanes(vec(b_router[l // 2])))
        tops = [None, None]
        for grp in range(2):
            x = xs[grp]
            m = x.shape[0]
            tm = tiles[grp]
            stacked = (seq, kv_stack[0], kv_stack[1]) if grp == 0 else None
            q, k, v, kb, vb, lf_pad, u = _proj_in(x, vec(g_mix[l]), wqkv, wf, wglu, bf, tm, stacked)
            lf = lf_pad[:, :N_HEADS]
            if grp == 0:
                c = _cumsum_logf(lf_pad, nb, seq)
                o_attn = _fox_prompt(q, kb, vb, c, nb, seq, _row_tile(seq, 512), _row_tile(seq, 512))
                u3 = u.reshape(nb, seq, cw)
                o_conv = _conv_prompt(u3, *conv_args, _row_tile(seq, 512)).reshape(m, cw)
                kv_stack = (k, v)
                outs["lfp"].append(lf.reshape(nb, seq, N_HEADS))
                outs["cvp"].append(u3[:, seq - hist:])
                n_grp, t_grp = nb, seq
            else:
                new_t = lambda z: jnp.pad(z.reshape(nd, t_new, -1).transpose(0, 2, 1),
                                          ((0, 0), (0, 0), (0, page - t_new)))
                o_attn = _fox_sample(page_table, l * n_phys, q, new_t(k).reshape(nd * a, page),
                                     new_t(v).reshape(nd * a, page), new_t(lf),
                                     cache_kt, cache_vt, cache_lft, t_new)
                u_hist = jnp.concatenate([cache_conv[l].astype(F32), u.reshape(nd, t_new, cw)], axis=1)
                o_conv = _conv_sample(u_hist, *conv_args, t_new).reshape(m, cw)
                outs["ks"].append(k.reshape(nd, t_new, N_HEADS, HEAD_DIM))
                outs["vs"].append(v.reshape(nd, t_new, N_HEADS, HEAD_DIM))
                outs["lfs"].append(lf.reshape(nd, t_new, N_HEADS))
                outs["cvs"].append(u_hist[:, -hist:])
                n_grp, t_grp = nd, t_new
            xs[grp], tops[grp] = _mix_cross(x, o_attn, o_conv, vec(g_attn_out[l]), wa, wc, vec(g_cross[l]), wq,
                                            mems[grp][0], mems[grp][1], wo, n_grp, mems[grp][2],
                                            _row_tile(t_grp, 512), router)

        if l % 2 == 0:
            wgu_d, wd_d = w_ff_gu[l // 2].astype(BF16), w_ff_down[l // 2].astype(BF16)
            xs = [_ffn(x, g_f, wgu_d, wd_d, tm, 256) for x, tm in zip(xs, tiles)]
            if last:
                xs = [_final_norm(x, vec(g_final), tm) for x, tm in zip(xs, tiles)]
        else:
            e = l // 2
            slots, tile_expert, tile_rows, n_slots = _moe_routing(tops, MOE_TILE)
            slot_rows = jnp.zeros((n_slots * SUBLANES, LANES), F32)
            for x, (s1, s2) in zip(xs, slots):
                slot_rows = _moe_dispatch(s1, s2, x, g_f, slot_rows, _row_tile(x.shape[0], 256))
            ys = _moe_sparse(tile_expert, tile_rows, slot_rows, w_e_gu[e], w_e_down[e], MOE_TILE, 512)
            xs = [_moe_combine(s1, s2, x, top, vec(g_final), ys, last, _row_tile(x.shape[0], 256))
                  for x, top, (s1, s2) in zip(xs, tops, slots)]

    st = lambda key: jnp.stack(outs[key])
    kv_prompt = lambda z: z.reshape(depth, nb, N_HEADS, HEAD_DIM, seq).transpose(0, 1, 4, 2, 3)
    return (xs[0].reshape(nb, seq, d), xs[1].reshape(nd, t_new, d),
            kv_prompt(kv_stack[0]), kv_prompt(kv_stack[1]), st("lfp"), st("cvp"), st("mkp"), st("mvp"),
            st("ks"), st("vs"), st("lfs"), st("cvs"))
```
